```python
import jax, jax.numpy as jnp
from jax import lax
import numpy as np

D_MODEL = 1024
BATCH = 16
SEQ = 2048
DEPTH = 1

CTX_LEN = 256
GRID_W = 64
D_MIX = D_MODEL
HEAD_DIM = 64
D_ATTN = D_MIX // 2
N_Q_HEADS = D_ATTN // HEAD_DIM
N_KV_HEADS = 2
Q_PER_KV = N_Q_HEADS // N_KV_HEADS
D_KV = N_KV_HEADS * HEAD_DIM
D_LRU = D_MIX - D_ATTN
LRU_BLOCKS = 8
LRU_BLOCK = D_LRU // LRU_BLOCKS
CONV_W = 4
CONV_PAD_LEFT = 2
RG_C = 8.0
ROPE_THETA = 10000.0
ROPE_FREQS = HEAD_DIM // 4
Q_BLOCK = 128
D_IN = D_ATTN + 2 * D_KV + 2 * D_LRU
PEER_HEADS = 8
PEER_NKEYS = 128
PEER_N = PEER_NKEYS * PEER_NKEYS
PEER_DKEY = 256
PEER_TOPK = 16
PEER_CHUNK = 128
N_MOD = 6
EPS = 1e-6

kernel_name = "hybrid_rglru_gqa_peer_prefix_dit"


def rmsnorm(x, g):
    x32 = x.astype(jnp.float32)
    y = x32 * lax.rsqrt(jnp.mean(x32 * x32, axis=-1, keepdims=True) + EPS)
    return (y * g.astype(jnp.float32)).astype(x.dtype)


def modulate(h, shift, scale):
    return h * (1.0 + scale) + shift


def split_heads(t, n):
    return t.reshape(t.shape[:-1] + (n, HEAD_DIM))


def group_q(q):
    return q.reshape(q.shape[:2] + (N_KV_HEADS, Q_PER_KV, HEAD_DIM))


def axial_rope(x, row, col):
    inv = ROPE_THETA ** (-jnp.arange(ROPE_FREQS, dtype=jnp.float32) / ROPE_FREQS)
    ang = jnp.stack([row.astype(jnp.float32)[:, None] * inv,
                     col.astype(jnp.float32)[:, None] * inv], axis=1)
    cos = jnp.cos(ang)[None, :, None].astype(x.dtype)
    sin = jnp.sin(ang)[None, :, None].astype(x.dtype)
    xp = x.reshape(x.shape[:-1] + (2, 2, ROPE_FREQS))
    x1, x2 = xp[..., 0, :], xp[..., 1, :]
    out = jnp.stack([x1 * cos - x2 * sin, x1 * sin + x2 * cos], axis=-2)
    return out.reshape(x.shape)


def attend(q, k, v):
    s = jnp.einsum('bqkgd,bskd->bkgqs', q, k).astype(jnp.float32) * (HEAD_DIM ** -0.5)
    p = jax.nn.softmax(s, axis=-1).astype(v.dtype)
    return jnp.einsum('bkgqs,bskd->bqkgd', p, v)


def latent_attention(q, k_all, v_all):
    B, T = q.shape[:2]
    nb = T // Q_BLOCK
    qb = jnp.moveaxis(q.reshape(B, nb, Q_BLOCK, N_KV_HEADS, Q_PER_KV, HEAD_DIM), 1, 0)
    ob = lax.map(lambda qq: attend(qq, k_all, v_all), qb)
    return jnp.moveaxis(ob, 0, 1).reshape(B, T, D_ATTN)


def dwconv(x, w, b):
    T = x.shape[1]
    xp = jnp.pad(x, ((0, 0), (CONV_PAD_LEFT, CONV_W - 1 - CONV_PAD_LEFT), (0, 0)))
    return sum(xp[:, j:j + T] * w[j] for j in range(CONV_W)) + b


def rglru_coeffs(x, wa, ba, wi, bi, lam):
    xb = x.reshape(x.shape[:2] + (LRU_BLOCKS, LRU_BLOCK))
    r = jax.nn.sigmoid(jnp.einsum('btni,nij->btnj', xb, wa).reshape(x.shape) + ba)
    i = jax.nn.sigmoid(jnp.einsum('btni,nij->btnj', xb, wi).reshape(x.shape) + bi)
    log_a = -RG_C * r.astype(jnp.float32) * jax.nn.softplus(-lam.astype(jnp.float32))
    a = jnp.exp(log_a)
    b = jnp.sqrt(jnp.maximum(-jnp.expm1(2.0 * log_a), 0.0)) * (i * x).astype(jnp.float32)
    return a, b


def linear_scan(a, b, reverse):
    def combine(l, r):
        return l[0] * r[0], r[0] * l[1] + r[1]
    return lax.associative_scan(combine, (a, b), axis=1, reverse=reverse)[1]


def rglru_bidir(xc, xl, wa, ba, wi, bi, lam):
    yc, yl = 0.0, 0.0
    for d, reverse in ((0, False), (1, True)):
        ac, bc = rglru_coeffs(xc, wa[d], ba[d], wi[d], bi[d], lam[d])
        hc = linear_scan(ac, bc, reverse)
        h0 = hc[:, 0] if reverse else hc[:, -1]
        al, bl = rglru_coeffs(xl, wa[d], ba[d], wi[d], bi[d], lam[d])
        edge = -1 if reverse else 0
        bl = bl.at[:, edge].add(al[:, edge] * h0)
        hl = linear_scan(al, bl, reverse)
        yc = yc + hc
        yl = yl + hl
    return yc.astype(xc.dtype), yl.astype(xl.dtype)


def peer(h, wq, subkeys, u, v):
    shp = h.shape
    chunks = h.reshape(-1, PEER_CHUNK, D_MODEL)

    def one(xc):
        q = (xc @ wq).reshape(PEER_CHUNK, PEER_HEADS, 2, PEER_DKEY // 2)
        s = jnp.einsum('nhpd,hpkd->nhpk', q, subkeys).astype(jnp.float32)
        s1, i1 = lax.top_k(s[:, :, 0], PEER_TOPK)
        s2, i2 = lax.top_k(s[:, :, 1], PEER_TOPK)
        cand_s = (s1[..., :, None] + s2[..., None, :]).reshape(PEER_CHUNK, PEER_HEADS, PEER_TOPK * PEER_TOPK)
        cand_i = (i1[..., :, None] * PEER_NKEYS + i2[..., None, :]).reshape(PEER_CHUNK, PEER_HEADS, PEER_TOPK * PEER_TOPK)
        top_s, pos = lax.top_k(cand_s, PEER_TOPK)
        idx = jnp.take_along_axis(cand_i, pos, axis=-1)
        g = jax.nn.softmax(top_s, axis=-1).astype(xc.dtype)
        act = jax.nn.gelu(jnp.einsum('nhkd,nd->nhk', u[idx], xc))
        return jnp.einsum('nhk,nhkd->nd', g * act, v[idx])

    return lax.map(one, chunks).reshape(shp)


def setup_inputs(seed: int = 0) -> dict:
    key = jax.random.key(seed)
    ks = jax.random.split(key, 26)
    f32 = jnp.float32

    def nrm(k, shape, scale):
        return jax.random.normal(k, shape, f32) * scale

    def gain(k, n):
        return 1.0 + 0.02 * jax.random.normal(k, (DEPTH, n), f32)

    a0 = jax.random.uniform(ks[25], (DEPTH, 2, D_LRU), f32, 0.9, 0.999)
    rg_lambda = jnp.log(a0) - jnp.log1p(-a0)
    return {
        "x": nrm(ks[0], (BATCH, SEQ, D_MODEL), 1.0),
        "c": nrm(ks[1], (BATCH, D_MODEL), 1.0),
        "ctx": nrm(ks[2], (BATCH, CTX_LEN, D_MODEL), 1.0),
        "c_ctx": nrm(ks[3], (D_MODEL,), 1.0),
        "w_mod": nrm(ks[4], (DEPTH, D_MODEL, N_MOD * D_MODEL), 0.5 * D_MODEL ** -0.5),
        "b_mod": nrm(ks[5], (DEPTH, N_MOD * D_MODEL), 0.01),
        "g_pre_mix": gain(ks[6], D_MODEL),
        "g_post_mix": gain(ks[7], D_MODEL),
        "g_pre_ffn": gain(ks[8], D_MODEL),
        "g_post_ffn": gain(ks[9], D_MODEL),
        "w_in": nrm(ks[10], (DEPTH, D_MODEL, D_IN), D_MODEL ** -0.5),
        "q_norm_g": gain(ks[11], HEAD_DIM),
        "k_norm_g": gain(ks[12], HEAD_DIM),
        "conv_w": nrm(ks[13], (DEPTH, CONV_W, D_LRU), CONV_W ** -0.5),
        "conv_b": nrm(ks[14], (DEPTH, D_LRU), 0.01),
        "rg_wa": nrm(ks[15], (DEPTH, 2, LRU_BLOCKS, LRU_BLOCK, LRU_BLOCK), LRU_BLOCK ** -0.5),
        "rg_ba": nrm(ks[16], (DEPTH, 2, D_LRU), 0.01),
        "rg_wi": nrm(ks[17], (DEPTH, 2, LRU_BLOCKS, LRU_BLOCK, LRU_BLOCK), LRU_BLOCK ** -0.5),
        "rg_bi": nrm(ks[18], (DEPTH, 2, D_LRU), 0.01),
        "rg_lambda": rg_lambda,
        "w_out": nrm(ks[19], (DEPTH, D_MIX, D_MODEL), D_MIX ** -0.5),
        "peer_wq": nrm(ks[20], (DEPTH, D_MODEL, PEER_HEADS * PEER_DKEY), D_MODEL ** -0.5),
        "peer_subkeys": nrm(ks[21], (DEPTH, PEER_HEADS, 2, PEER_NKEYS, PEER_DKEY // 2), (PEER_DKEY // 2) ** -0.5),
        "peer_u": nrm(ks[22], (DEPTH, PEER_N, D_MODEL), D_MODEL ** -0.5),
        "peer_v": nrm(ks[23], (DEPTH, PEER_N, D_MODEL), D_MODEL ** -0.5),
    }


def reference(x, c, ctx, c_ctx, w_mod, b_mod, g_pre_mix, g_post_mix, g_pre_ffn, g_post_ffn,
              w_in, q_norm_g, k_norm_g, conv_w, conv_b, rg_wa, rg_ba, rg_wi, rg_bi, rg_lambda,
              w_out, peer_wq, peer_subkeys, peer_u, peer_v):
    B, T, _ = x.shape
    C = ctx.shape[1]
    ROWS = T // GRID_W
    row = jnp.repeat(jnp.arange(ROWS, dtype=jnp.int32), GRID_W)
    col = jnp.tile(jnp.arange(GRID_W, dtype=jnp.int32), ROWS)
    silu_c = jax.nn.silu(c)
    silu_cc = jax.nn.silu(c_ctx)
    cuts = [D_ATTN, D_ATTN + D_KV, D_ATTN + 2 * D_KV, D_ATTN + 2 * D_KV + D_LRU]

    for l in range(DEPTH):
        last = l == DEPTH - 1
        mod_x = (silu_c @ w_mod[l] + b_mod[l]).reshape(B, N_MOD, 1, D_MODEL)
        mod_c = (silu_cc @ w_mod[l] + b_mod[l]).reshape(N_MOD, 1, D_MODEL)
        sh1, sc1, g1, sh2, sc2, g2 = (mod_x[:, i] for i in range(N_MOD))
        csh1, csc1, cg1, csh2, csc2, cg2 = (mod_c[i] for i in range(N_MOD))

        px = modulate(rmsnorm(x, g_pre_mix[l]), sh1, sc1) @ w_in[l]
        pc = modulate(rmsnorm(ctx, g_pre_mix[l]), csh1, csc1) @ w_in[l]
        qx, kx, vx, lx, zx = jnp.split(px, cuts, axis=-1)
        qc, kc, vc, lc, zc = jnp.split(pc, cuts, axis=-1)

        qx = axial_rope(rmsnorm(split_heads(qx, N_Q_HEADS), q_norm_g[l]), row, col)
        kx = axial_rope(rmsnorm(split_heads(kx, N_KV_HEADS), k_norm_g[l]), row, col)
        kc = rmsnorm(split_heads(kc, N_KV_HEADS), k_norm_g[l])
        vc = split_heads(vc, N_KV_HEADS)
        k_all = jnp.concatenate([kc, kx], axis=1)
        v_all = jnp.concatenate([vc, split_heads(vx, N_KV_HEADS)], axis=1)
        attn_x = latent_attention(group_q(qx), k_all, v_all)

        yc, yl = rglru_bidir(dwconv(lc, conv_w[l], conv_b[l]), dwconv(lx, conv_w[l], conv_b[l]),
                             rg_wa[l], rg_ba[l], rg_wi[l], rg_bi[l], rg_lambda[l])
        lru_x = yl * jax.nn.gelu(zx)

        mix_x = jnp.concatenate([attn_x, lru_x], axis=-1) @ w_out[l]
        x_mid = x + g1 * rmsnorm(mix_x, g_post_mix[l])

        if not last:
            qc = rmsnorm(split_heads(qc, N_Q_HEADS), q_norm_g[l])
            attn_c = attend(group_q(qc), kc, vc).reshape(B, C, D_ATTN)
            lru_c = yc * jax.nn.gelu(zc)
            mix_c = jnp.concatenate([attn_c, lru_c], axis=-1) @ w_out[l]
            ctx = ctx + cg1 * rmsnorm(mix_c, g_post_mix[l])
            hc2 = modulate(rmsnorm(ctx, g_pre_ffn[l]), csh2, csc2)
            ctx = ctx + cg2 * rmsnorm(peer(hc2, peer_wq[l], peer_subkeys[l], peer_u[l], peer_v[l]), g_post_ffn[l])

        hx2 = modulate(rmsnorm(x_mid, g_pre_ffn[l]), sh2, sc2)
        ffn_x = peer(hx2, peer_wq[l], peer_subkeys[l], peer_u[l], peer_v[l])
        x = x_mid + g2 * rmsnorm(ffn_x, g_post_ffn[l])

    return x
```

```python
import functools

import jax
import jax.numpy as jnp
import numpy as np
from jax import lax
from jax.experimental import pallas as pl
from jax.experimental.pallas import tpu as pltpu

F32 = jnp.float32
BF16 = jnp.bfloat16

HEAD_DIM = 64
N_Q_HEADS = 8
N_KV_HEADS = 2
D_ATTN = 512
D_KV = 128
D_LRU = 512
LRU_BLOCKS = 8
CONV_W = 4
CONV_PAD_LEFT = 2
RG_C = 8.0
ROPE_THETA = 10000.0
ROPE_FREQS = 16
GRID_W = 64
N_MOD = 6
EPS = 1e-6
PEER_HEADS = 8
PEER_NKEYS = 128
PEER_TOPK = 16
N_SEL = PEER_HEADS * PEER_TOPK

SUBLANES = 8
LANES = 128
VMEM_LIMIT = 56 * 1024 * 1024

NT_DIMS = (((1,), (1,)), ((), ()))


def _dot(a, b):
    return jnp.dot(a, b, preferred_element_type=F32)


def _dot_nt(a, b):
    return lax.dot_general(a, b, NT_DIMS, preferred_element_type=F32)


def _split_bf16(x):
    hi = x.astype(BF16)
    return hi, (x - hi.astype(F32)).astype(BF16)


def _rms_scale(x):
    return lax.rsqrt(jnp.mean(x * x, axis=-1, keepdims=True) + EPS)


def _mod_kernel(c_ref, w_ref, b_ref, o_ref):
    s = jax.nn.silu(c_ref[...])
    s_hi, s_lo = _split_bf16(s)
    w_hi, w_lo = _split_bf16(w_ref[...])
    o_ref[...] = _dot(s_hi, w_hi) + _dot(s_lo, w_hi) + _dot(s_hi, w_lo) + b_ref[...]


def _modulation(c_all, w_mod, b_mod):
    m, d = c_all.shape
    n = w_mod.shape[1]
    tn = 1536
    return pl.pallas_call(
        _mod_kernel,
        grid=(n // tn,),
        in_specs=[pl.BlockSpec((m, d), lambda j: (0, 0)),
                  pl.BlockSpec((d, tn), lambda j: (0, j)),
                  pl.BlockSpec((1, tn), lambda j: (0, j))],
        out_specs=pl.BlockSpec((m, tn), lambda j: (0, j)),
        out_shape=jax.ShapeDtypeStruct((m, n), F32),
        compiler_params=pltpu.CompilerParams(dimension_semantics=("arbitrary",), vmem_limit_bytes=VMEM_LIMIT),
        name="modulation",
    )(c_all, w_mod, b_mod.reshape(1, n))


def _head_rms(y, gain, bd):
    hi, lo = _split_bf16(y * y)
    ss = _dot(hi, bd) + _dot(lo, bd)
    return y * lax.rsqrt(ss * (1.0 / HEAD_DIM) + EPS) * gain


def _rope(y, cos, sin):
    n = y.shape[-1]
    lane = lax.broadcasted_iota(jnp.int32, y.shape, 1)
    first_half = (lane % (2 * ROPE_FREQS)) < ROPE_FREQS
    swapped = jnp.where(first_half, pltpu.roll(y, n - ROPE_FREQS, 1), pltpu.roll(y, ROPE_FREQS, 1))
    return y * cos + swapped * sin


def _inproj_latent_kernel(x_ref, sh_ref, sc_ref, g_ref, w_ref, cos_ref, sin_ref, qg_ref, kg_ref, bd_ref,
                          q_ref, k_ref, v_ref, l_ref, gz_ref):
    x = x_ref[0]
    h = x * _rms_scale(x) * g_ref[...] * (1.0 + sc_ref[0]) + sh_ref[0]
    p = _dot(h.astype(BF16), w_ref[...])
    cos, sin = cos_ref[...], sin_ref[...]
    bd = bd_ref[...]
    q = _head_rms(p[:, :D_ATTN], qg_ref[...], bd)
    q = _rope(q, jnp.concatenate([cos] * 4, axis=1), jnp.concatenate([sin] * 4, axis=1))
    q_ref[0] = (q * (HEAD_DIM ** -0.5)).astype(BF16)
    k = _head_rms(p[:, D_ATTN:D_ATTN + D_KV], kg_ref[...], bd[:D_KV, :D_KV])
    k_ref[0] = _rope(k, cos, sin).astype(BF16)
    v_ref[0] = p[:, D_ATTN + D_KV:D_ATTN + 2 * D_KV].astype(BF16)
    l_ref[0] = p[:, D_ATTN + 2 * D_KV:D_ATTN + 2 * D_KV + D_LRU]
    gz_ref[0] = jax.nn.gelu(p[:, D_ATTN + 2 * D_KV + D_LRU:])


def _inproj_context_kernel(x_ref, sh_ref, sc_ref, g_ref, w_ref, kg_ref, bd_ref, k_ref, v_ref, l_ref):
    x = x_ref[0]
    h = x * _rms_scale(x) * g_ref[...] * (1.0 + sc_ref[...]) + sh_ref[...]
    p = _dot(h.astype(BF16), w_ref[...])
    k_ref[0] = _head_rms(p[:, :D_KV], kg_ref[...], bd_ref[...]).astype(BF16)
    v_ref[0] = p[:, D_KV:2 * D_KV].astype(BF16)
    l_ref[0] = p[:, 2 * D_KV:]


def _full(shape):
    nd = len(shape)
    return pl.BlockSpec(shape, lambda *_: (0,) * nd)


def _inproj_latent(x, sh, sc, gpre, w_bf, cos, sin, qg, kg, bd, tm):
    B, T, D = x.shape
    tok = lambda n: pl.BlockSpec((1, tm, n), lambda b, i: (b, i, 0))
    vec = pl.BlockSpec((1, 1, D), lambda b, i: (b, 0, 0))
    rope = pl.BlockSpec((tm, LANES), lambda b, i: (i, 0))
    return pl.pallas_call(
        _inproj_latent_kernel,
        grid=(B, T // tm),
        in_specs=[tok(D), vec, vec, _full((1, D)), _full(w_bf.shape), rope, rope,
                  _full((1, D_ATTN)), _full((1, D_KV)), _full((D_ATTN, D_ATTN))],
        out_specs=[tok(D_ATTN), tok(D_KV), tok(D_KV), tok(D_LRU), tok(D_LRU)],
        out_shape=[jax.ShapeDtypeStruct((B, T, D_ATTN), BF16), jax.ShapeDtypeStruct((B, T, D_KV), BF16),
                   jax.ShapeDtypeStruct((B, T, D_KV), BF16), jax.ShapeDtypeStruct((B, T, D_LRU), F32),
                   jax.ShapeDtypeStruct((B, T, D_LRU), F32)],
        compiler_params=pltpu.CompilerParams(dimension_semantics=("arbitrary", "arbitrary"),
                                             vmem_limit_bytes=VMEM_LIMIT),
        name="inproj_latent",
    )(x, sh, sc, gpre, w_bf, cos, sin, qg, kg, bd)


def _inproj_context(ctx, sh, sc, gpre, w_bf, kg, bd, tm):
    B, C, D = ctx.shape
    tok = lambda n: pl.BlockSpec((1, tm, n), lambda b, i: (b, i, 0))
    return pl.pallas_call(
        _inproj_context_kernel,
        grid=(B, C // tm),
        in_specs=[tok(D), _full((1, D)), _full((1, D)), _full((1, D)), _full(w_bf.shape),
                  _full((1, D_KV)), _full((D_KV, D_KV))],
        out_specs=[tok(D_KV), tok(D_KV), tok(D_LRU)],
        out_shape=[jax.ShapeDtypeStruct((B, C, D_KV), BF16), jax.ShapeDtypeStruct((B, C, D_KV), BF16),
                   jax.ShapeDtypeStruct((B, C, D_LRU), F32)],
        compiler_params=pltpu.CompilerParams(dimension_semantics=("arbitrary", "arbitrary"),
                                             vmem_limit_bytes=VMEM_LIMIT),
        name="inproj_context",
    )(ctx, sh, sc, gpre, w_bf, kg, bd)


GATE_ROWS = 256
HALO = SUBLANES


def _conv_chunk(src_ref, start, n_rows, cw, cb):
    lo, hi = start - HALO, start + GATE_ROWS + HALO
    parts = []
    if lo < 0:
        parts.append(jnp.zeros((HALO, D_LRU), F32))
        lo = 0
    body_hi = min(hi, n_rows)
    parts.append(src_ref[0, lo:body_hi, :])
    if hi > n_rows:
        parts.append(jnp.zeros((HALO, D_LRU), F32))
    xh = jnp.concatenate(parts, axis=0) if len(parts) > 1 else parts[0]
    out = cb
    for j in range(CONV_W):
        off = HALO + j - CONV_PAD_LEFT
        out = out + xh[off:off + GATE_ROWS, :] * cw[j:j + 1, :]
    return out


def _tile_scan(a, b, carry, row, reverse):
    for sh in (1, 2, 4):
        if reverse:
            a_n, b_n = pltpu.roll(a, SUBLANES - sh, 0), pltpu.roll(b, SUBLANES - sh, 0)
            valid = row < SUBLANES - sh
        else:
            a_n, b_n = pltpu.roll(a, sh, 0), pltpu.roll(b, sh, 0)
            valid = row >= sh
        b = jnp.where(valid, a * b_n + b, b)
        a = jnp.where(valid, a * a_n, a)
    h = a * carry + b
    new_carry = h[0:1, :] if reverse else h[SUBLANES - 1:SUBLANES, :]
    return h, new_carry


def _rglru_kernel(lc_ref, lx_ref, gz_ref, cw_ref, cb_ref, wa_ref, wi_ref, ba_ref, bi_ref, lam_ref, o_ref,
                  a0_ref, b0_ref, a1_ref, b1_ref, *, n_ctx, n_lat):
    cw, cb = cw_ref[...], cb_ref[...]
    a_refs, b_refs = (a0_ref, a1_ref), (b0_ref, b1_ref)
    for seg_ref, seg_rows, seg_off in ((lc_ref, n_ctx, 0), (lx_ref, n_lat, n_ctx)):
        for s in range(0, seg_rows, GATE_ROWS):
            xc = _conv_chunk(seg_ref, s, seg_rows, cw, cb)
            xb = xc.astype(BF16)
            for d in range(2):
                r = jax.nn.sigmoid(_dot(xb, wa_ref[d]) + ba_ref[d])
                i = jax.nn.sigmoid(_dot(xb, wi_ref[d]) + bi_ref[d])
                log_a = -RG_C * r * jax.nn.softplus(-lam_ref[d])
                a = jnp.exp(log_a)
                th = jnp.tanh(log_a)
                b = jnp.sqrt(jnp.maximum(-2.0 * th / (1.0 - th), 0.0)) * (i * xc)
                a_refs[d][seg_off + s:seg_off + s + GATE_ROWS, :] = a
                b_refs[d][seg_off + s:seg_off + s + GATE_ROWS, :] = b

    ctx_tiles, all_tiles = n_ctx // SUBLANES, (n_ctx + n_lat) // SUBLANES
    row = lax.broadcasted_iota(jnp.int32, (SUBLANES, D_LRU), 0)

    def step(j, carry):
        cf, cr = carry
        tf = pl.multiple_of(j * SUBLANES, SUBLANES)
        jr = jnp.where(j < ctx_tiles, ctx_tiles - 1 - j, all_tiles + ctx_tiles - 1 - j)
        tr = pl.multiple_of(jr * SUBLANES, SUBLANES)
        hf, cf = _tile_scan(a0_ref[pl.ds(tf, SUBLANES), :], b0_ref[pl.ds(tf, SUBLANES), :], cf, row, False)
        hr, cr = _tile_scan(a1_ref[pl.ds(tr, SUBLANES), :], b1_ref[pl.ds(tr, SUBLANES), :], cr, row, True)
        b0_ref[pl.ds(tf, SUBLANES), :] = hf
        b1_ref[pl.ds(tr, SUBLANES), :] = hr
        return cf, cr

    zero = jnp.zeros((1, D_LRU), F32)
    lax.fori_loop(0, all_tiles, step, (zero, zero))
    for s in range(0, n_lat, GATE_ROWS):
        y = b0_ref[n_ctx + s:n_ctx + s + GATE_ROWS, :] + b1_ref[n_ctx + s:n_ctx + s + GATE_ROWS, :]
        o_ref[0, s:s + GATE_ROWS, :] = (y * gz_ref[0, s:s + GATE_ROWS, :]).astype(BF16)


def _rglru(lc, lx, gz, conv_w, conv_b, wa_bd, wi_bd, ba, bi, lam):
    B, C, _ = lc.shape
    T = lx.shape[1]
    seq = lambda n: pl.BlockSpec((1, n, D_LRU), lambda b: (b, 0, 0))
    scratch = [pltpu.VMEM((C + T, D_LRU), F32) for _ in range(4)]
    return pl.pallas_call(
        functools.partial(_rglru_kernel, n_ctx=C, n_lat=T),
        grid=(B,),
        in_specs=[seq(C), seq(T), seq(T), _full((CONV_W, D_LRU)), _full((1, D_LRU)),
                  _full((2, D_LRU, D_LRU)), _full((2, D_LRU, D_LRU)),
                  _full((2, 1, D_LRU)), _full((2, 1, D_LRU)), _full((2, 1, D_LRU))],
        out_specs=seq(T),
        out_shape=jax.ShapeDtypeStruct((B, T, D_LRU), BF16),
        scratch_shapes=scratch,
        compiler_params=pltpu.CompilerParams(dimension_semantics=("arbitrary",), vmem_limit_bytes=VMEM_LIMIT),
        name="rglru",
    )(lc, lx, gz, conv_w, conv_b, wa_bd, wi_bd, ba, bi, lam)


def _attn_kernel(q_ref, k_ref, v_ref, o_ref, *, tq):
    group = N_Q_HEADS // N_KV_HEADS
    for g in range(N_KV_HEADS):
        q = q_ref[0, g * group:(g + 1) * group].reshape(group * tq, HEAD_DIM)
        s = _dot_nt(q, k_ref[0, g])
        p = jnp.exp(s - jnp.max(s, axis=-1, keepdims=True))
        denom = jnp.sum(p, axis=-1, keepdims=True)
        o = _dot(p.astype(BF16), v_ref[0, g]) / denom
        o_ref[0, g * group:(g + 1) * group] = o.reshape(group, tq, HEAD_DIM).astype(BF16)


def _attention(q4, k4, v4, tq):
    B, H, T, _ = q4.shape
    S = k4.shape[2]
    kv = pl.BlockSpec((1, N_KV_HEADS, S, HEAD_DIM), lambda b, i: (b, 0, 0, 0))
    qo = pl.BlockSpec((1, H, tq, HEAD_DIM), lambda b, i: (b, 0, i, 0))
    return pl.pallas_call(
        functools.partial(_attn_kernel, tq=tq),
        grid=(B, T // tq),
        in_specs=[qo, kv, kv],
        out_specs=qo,
        out_shape=jax.ShapeDtypeStruct((B, H, T, HEAD_DIM), BF16),
        compiler_params=pltpu.CompilerParams(dimension_semantics=("arbitrary", "arbitrary"),
                                             vmem_limit_bytes=VMEM_LIMIT),
        name="attention",
    )(q4, k4, v4)


def _outproj_kernel(attn_ref, lru_ref, x_ref, wout_ref, gpm_ref, g1_ref, gpf_ref, sh_ref, sc_ref, wq_ref,
                    xmid_ref, hx_ref, qp_ref):
    mix = _dot(attn_ref[0], wout_ref[:D_ATTN, :]) + _dot(lru_ref[0], wout_ref[D_ATTN:, :])
    x_mid = x_ref[0] + g1_ref[0] * (mix * _rms_scale(mix) * gpm_ref[...])
    xmid_ref[0] = x_mid
    hx = (x_mid * _rms_scale(x_mid) * gpf_ref[...] * (1.0 + sc_ref[0]) + sh_ref[0]).astype(BF16)
    hx_ref[0] = hx
    qp_ref[0] = _dot(hx, wq_ref[...]).astype(BF16)


def _outproj(attn, lru, x, wout_bf, gpm, g1, gpf, sh2, sc2, wq_bf, tm):
    B, T, D = x.shape
    nq = wq_bf.shape[1]
    tok = lambda n: pl.BlockSpec((1, tm, n), lambda b, i: (b, i, 0))
    vec = pl.BlockSpec((1, 1, D), lambda b, i: (b, 0, 0))
    return pl.pallas_call(
        _outproj_kernel,
        grid=(B, T // tm),
        in_specs=[tok(D_ATTN), tok(D_LRU), tok(D), _full(wout_bf.shape), _full((1, D)), vec, _full((1, D)),
                  vec, vec, _full(wq_bf.shape)],
        out_specs=[tok(D), tok(D), tok(nq)],
        out_shape=[jax.ShapeDtypeStruct((B, T, D), F32), jax.ShapeDtypeStruct((B, T, D), BF16),
                   jax.ShapeDtypeStruct((B, T, nq), BF16)],
        compiler_params=pltpu.CompilerParams(dimension_semantics=("arbitrary", "arbitrary"),
                                             vmem_limit_bytes=VMEM_LIMIT),
        name="outproj",
    )(attn, lru, x, wout_bf, gpm, g1, gpf, sh2, sc2, wq_bf)


NEG_INF = float("-inf")
BIG = 1 << 20


def _extract_topk(vals, code, k):
    n = vals.shape[1]
    slot = lax.broadcasted_iota(jnp.int32, (k, n), 0)
    top_v = jnp.zeros((k, n), F32)
    top_c = jnp.zeros((k, n), jnp.int32)
    for j in range(k):
        m = jnp.max(vals, axis=0, keepdims=True)
        c = jnp.min(jnp.where(vals == m, code, BIG), axis=0, keepdims=True)
        top_v = jnp.where(slot == j, m, top_v)
        top_c = jnp.where(slot == j, c, top_c)
        vals = jnp.where(code == c, NEG_INF, vals)
    return top_v, top_c


def _lookup(table, pos):
    out = jnp.zeros(pos.shape, table.dtype)
    for i in range(PEER_TOPK):
        out = jnp.where(pos == i, table[i:i + 1, :], out)
    return out


def _topk_kernel(qp_ref, keys_ref, ia_ref, ib_ref, g_ref, *, tn):
    key_id = lax.broadcasted_iota(jnp.int32, (PEER_NKEYS, tn), 0)
    half = SUBLANES
    sub = lax.broadcasted_iota(jnp.int32, (half, tn), 0)
    code_groups = [sub, sub + half] + [sub + 16 * i for i in range(1, half)] + [(sub + half) * 16]
    cand_code = jnp.concatenate(code_groups, axis=0)
    for h in range(PEER_HEADS):
        tops = []
        for p in range(2):
            col = (2 * h + p) * PEER_NKEYS
            s = _dot_nt(keys_ref[2 * h + p], qp_ref[:, col:col + PEER_NKEYS])
            tops.append(_extract_topk(s, key_id, PEER_TOPK))
        (v1, i1), (v2, i2) = tops
        groups = [v1[0:1, :] + v2[:half, :], v1[0:1, :] + v2[half:, :]]
        groups += [v1[i:i + 1, :] + v2[:half, :] for i in range(1, half)]
        groups += [v1[half:, :] + v2[0:1, :]]
        top_s, top_c = _extract_topk(jnp.concatenate(groups, axis=0), cand_code, PEER_TOPK)
        e = jnp.exp(top_s - top_s[0:1, :])
        rows = slice(h * PEER_TOPK, (h + 1) * PEER_TOPK)
        g_ref[rows, :] = e / jnp.sum(e, axis=0, keepdims=True)
        ia_ref[rows, :] = _lookup(i1, top_c >> 4)
        ib_ref[rows, :] = _lookup(i2, top_c & 15)


def _peer_topk(qp, keys_bf, tn):
    n, nq = qp.shape
    out = pl.BlockSpec((N_SEL, tn), lambda i: (0, i))
    return pl.pallas_call(
        functools.partial(_topk_kernel, tn=tn),
        grid=(n // tn,),
        in_specs=[pl.BlockSpec((tn, nq), lambda i: (i, 0)), _full(keys_bf.shape)],
        out_specs=[out, out, out],
        out_shape=[jax.ShapeDtypeStruct((N_SEL, n), jnp.int32), jax.ShapeDtypeStruct((N_SEL, n), jnp.int32),
                   jax.ShapeDtypeStruct((N_SEL, n), F32)],
        compiler_params=pltpu.CompilerParams(dimension_semantics=("arbitrary",), vmem_limit_bytes=VMEM_LIMIT),
        name="peer_topk",
    )(qp, keys_bf)


EXPERT_CHUNK = 1024
GROUPS_PER_CHUNK = EXPERT_CHUNK // PEER_NKEYS
TOKEN_UNROLL = 4


def _peer_kernel(hx_ref, ia_ref, ib_ref, g_ref, ut_ref, v_ref, xmid_ref, g2_ref, gpost_ref, o_ref,
                 s_ref, acc_ref, *, tb, n_chunks):
    j = pl.program_id(1)
    srow = tb + SUBLANES

    def group_rows(gidx):
        return pl.ds(pl.multiple_of(gidx * srow, SUBLANES), tb)

    @pl.when(j < n_chunks)
    def _scores():
        res = _dot(hx_ref[...], ut_ref[...])
        for l in range(GROUPS_PER_CHUNK):
            s_ref[group_rows(j * GROUPS_PER_CHUNK + l), :] = res[:, l * PEER_NKEYS:(l + 1) * PEER_NKEYS]

    @pl.when(j == n_chunks)
    def _select():
        sub = lax.broadcasted_iota(jnp.int32, (PEER_NKEYS, N_SEL), 0)

        def tokens(tt, carry):
            for u in range(TOKEN_UNROLL):
                t = tt * TOKEN_UNROLL + u
                first = jnp.where(sub == ia_ref[t], g_ref[t], 0.0).astype(BF16)
                second = jnp.where(sub == ib_ref[t], 1.0, 0.0).astype(BF16)
                wd = _dot_nt(first, second)
                rows = pl.ds(t, PEER_NKEYS, stride=srow)
                s_ref[rows, :] = wd * jax.nn.gelu(s_ref[rows, :])
            return carry

        lax.fori_loop(0, tb // TOKEN_UNROLL, tokens, 0)
        acc_ref[...] = jnp.zeros_like(acc_ref)

    @pl.when(j >= n_chunks)
    def _combine():
        jj = j - n_chunks
        h = jnp.concatenate([s_ref[group_rows(jj * GROUPS_PER_CHUNK + l), :] for l in range(GROUPS_PER_CHUNK)],
                            axis=1)
        acc_ref[...] += _dot(h.astype(BF16), v_ref[...])

    @pl.when(j == 2 * n_chunks - 1)
    def _finish():
        f = acc_ref[...]
        o_ref[...] = xmid_ref[...] + g2_ref[0] * (f * _rms_scale(f) * gpost_ref[...])


def _peer_dense(hx, ia3, ib3, g3, ut_bf, v_bf, xmid, g2, gpost, tb, tokens_per_batch):
    n, d = hx.shape
    n_exp = v_bf.shape[0]
    n_chunks = n_exp // EXPERT_CHUNK
    bpb = tokens_per_batch // tb
    tok = pl.BlockSpec((tb, d), lambda i, j: (i, 0))
    pick = pl.BlockSpec((tb, 1, N_SEL), lambda i, j: (i, 0, 0))
    return pl.pallas_call(
        functools.partial(_peer_kernel, tb=tb, n_chunks=n_chunks),
        grid=(n // tb, 2 * n_chunks),
        in_specs=[tok, pick, pick, pick,
                  pl.BlockSpec((d, EXPERT_CHUNK), lambda i, j: (0, jnp.minimum(j, n_chunks - 1))),
                  pl.BlockSpec((EXPERT_CHUNK, d), lambda i, j: (jnp.maximum(j - n_chunks, 0), 0)),
                  tok, pl.BlockSpec((1, 1, d), lambda i, j: (i // bpb, 0, 0)), _full((1, d))],
        out_specs=tok,
        out_shape=jax.ShapeDtypeStruct((n, d), F32),
        scratch_shapes=[pltpu.VMEM((PEER_NKEYS * (tb + SUBLANES), PEER_NKEYS), F32), pltpu.VMEM((tb, d), F32)],
        compiler_params=pltpu.CompilerParams(dimension_semantics=("arbitrary", "arbitrary"),
                                             vmem_limit_bytes=VMEM_LIMIT),
        name="peer_dense",
    )(hx, ia3, ib3, g3, ut_bf, v_bf, xmid, g2, gpost)


def _rope_tables(T):
    t = np.arange(T)
    inv = ROPE_THETA ** (-jnp.arange(ROPE_FREQS, dtype=F32) / ROPE_FREQS)
    ang_r = jnp.asarray(t // GRID_W, F32)[:, None] * inv
    ang_c = jnp.asarray(t % GRID_W, F32)[:, None] * inv
    cos = jnp.concatenate([jnp.cos(ang_r)] * 2 + [jnp.cos(ang_c)] * 2, axis=1)
    sin = jnp.concatenate([-jnp.sin(ang_r), jnp.sin(ang_r), -jnp.sin(ang_c), jnp.sin(ang_c)], axis=1)
    return jnp.concatenate([cos, cos], axis=1), jnp.concatenate([sin, sin], axis=1)


def _block_diag(w):
    n, m, _ = w.shape
    eye = jnp.eye(n, dtype=w.dtype)
    return (eye[:, None, :, None] * w[:, :, None, :]).reshape(n * m, n * m)


def kernel(x, c, ctx, c_ctx, w_mod, b_mod, g_pre_mix, g_post_mix, g_pre_ffn, g_post_ffn, w_in, q_norm_g, k_norm_g, conv_w, conv_b, rg_wa, rg_ba, rg_wi, rg_bi, rg_lambda, w_out, peer_wq, peer_subkeys, peer_u, peer_v):
    B, T, D = x.shape
    C = ctx.shape[1]
    depth = w_mod.shape[0]
    cos, sin = _rope_tables(T)
    head_ones = _block_diag(jnp.ones((N_Q_HEADS, HEAD_DIM, HEAD_DIM), BF16))
    mod_rows = -(-(B + 1) // SUBLANES) * SUBLANES

    for l in range(depth):
        assert l == depth - 1, "context-stream update between layers is not implemented"
        c_all = jnp.zeros((mod_rows, D), F32).at[:B].set(c).at[B].set(c_ctx)
        mod = _modulation(c_all, w_mod[l], b_mod[l])
        mx = mod[:B].reshape(B, N_MOD, 1, D)
        sh1, sc1, g1, sh2, sc2, g2 = (mx[:, i] for i in range(N_MOD))
        mc = mod[B].reshape(N_MOD, 1, D)

        w_bf = w_in[l].astype(BF16)
        qg = jnp.tile(q_norm_g[l], N_Q_HEADS).reshape(1, D_ATTN)
        kg = jnp.tile(k_norm_g[l], N_KV_HEADS).reshape(1, D_KV)
        gpre = g_pre_mix[l].reshape(1, D)
        q, kx, vx, lx, gz = _inproj_latent(x, sh1, sc1, gpre, w_bf, cos, sin, qg, kg, head_ones, tm=512)
        kc, vc, lc = _inproj_context(ctx, mc[0], mc[1], gpre, w_bf[:, D_ATTN:D_ATTN + 2 * D_KV + D_LRU],
                                     kg, head_ones[:D_KV, :D_KV], tm=C)

        lru = _rglru(lc, lx, gz, conv_w[l], conv_b[l].reshape(1, D_LRU),
                     jnp.stack([_block_diag(rg_wa[l, d]) for d in range(2)]).astype(BF16),
                     jnp.stack([_block_diag(rg_wi[l, d]) for d in range(2)]).astype(BF16),
                     rg_ba[l].reshape(2, 1, D_LRU), rg_bi[l].reshape(2, 1, D_LRU), rg_lambda[l].reshape(2, 1, D_LRU))

        heads = lambda a, n: a.reshape(B, a.shape[1], n, HEAD_DIM).transpose(0, 2, 1, 3)
        k4 = heads(jnp.concatenate([kc, kx], axis=1), N_KV_HEADS)
        v4 = heads(jnp.concatenate([vc, vx], axis=1), N_KV_HEADS)
        attn = _attention(heads(q, N_Q_HEADS), k4, v4, tq=256).transpose(0, 2, 1, 3).reshape(B, T, D_ATTN)

        x_mid, hx, qp = _outproj(attn, lru, x, w_out[l].astype(BF16), g_post_mix[l].reshape(1, D), g1,
                                 g_pre_ffn[l].reshape(1, D), sh2, sc2, peer_wq[l].astype(BF16), tm=256)

        n = B * T
        keys_bf = peer_subkeys[l].reshape(2 * PEER_HEADS, PEER_NKEYS, -1).astype(BF16)
        ia, ib, gate = _peer_topk(qp.reshape(n, -1), keys_bf, tn=256)
        picks = lambda a: a.T.reshape(n, 1, N_SEL)
        x = _peer_dense(hx.reshape(n, D), picks(ia), picks(ib), picks(gate), peer_u[l].T.astype(BF16),
                        peer_v[l].astype(BF16), x_mid.reshape(n, D), g2, g_post_ffn[l].reshape(1, D),
                        tb=256, tokens_per_batch=T).reshape(B, T, D)
    return x
```

```python
import functools

import jax
import jax.numpy as jnp
import numpy as np
from jax import lax
from jax.experimental import pallas as pl
from jax.experimental.pallas import tpu as pltpu

F32 = jnp.float32
BF16 = jnp.bfloat16

HEAD_DIM = 64
N_Q_HEADS = 8
N_KV_HEADS = 2
D_ATTN = 512
D_KV = 128
D_LRU = 512
LRU_BLOCKS = 8
CONV_W = 4
CONV_PAD_LEFT = 2
RG_C = 8.0
ROPE_THETA = 10000.0
ROPE_FREQS = 16
GRID_W = 64
N_MOD = 6
EPS = 1e-6
PEER_HEADS = 8
PEER_NKEYS = 128
PEER_TOPK = 16
N_SEL = PEER_HEADS * PEER_TOPK

SUBLANES = 8
LANES = 128
VMEM_LIMIT = 56 * 1024 * 1024

NT_DIMS = (((1,), (1,)), ((), ()))


def _dot(a, b):
    return jnp.dot(a, b, preferred_element_type=F32)


def _dot_nt(a, b):
    return lax.dot_general(a, b, NT_DIMS, preferred_element_type=F32)


def _split_bf16(x):
    hi = x.astype(BF16)
    return hi, (x - hi.astype(F32)).astype(BF16)


def _rms_scale(x):
    return lax.rsqrt(jnp.mean(x * x, axis=-1, keepdims=True) + EPS)


def _mod_kernel(c_ref, w_ref, b_ref, o_ref):
    s = jax.nn.silu(c_ref[...])
    s_hi, s_lo = _split_bf16(s)
    w_hi, w_lo = _split_bf16(w_ref[...])
    o_ref[...] = _dot(s_hi, w_hi) + _dot(s_lo, w_hi) + _dot(s_hi, w_lo) + b_ref[...]


def _modulation(c_all, w_mod, b_mod):
    m, d = c_all.shape
    n = w_mod.shape[1]
    tn = 1536
    return pl.pallas_call(
        _mod_kernel,
        grid=(n // tn,),
        in_specs=[pl.BlockSpec((m, d), lambda j: (0, 0)),
                  pl.BlockSpec((d, tn), lambda j: (0, j)),
                  pl.BlockSpec((1, tn), lambda j: (0, j))],
        out_specs=pl.BlockSpec((m, tn), lambda j: (0, j)),
        out_shape=jax.ShapeDtypeStruct((m, n), F32),
        compiler_params=pltpu.CompilerParams(dimension_semantics=("arbitrary",), vmem_limit_bytes=VMEM_LIMIT),
        name="modulation",
    )(c_all, w_mod, b_mod.reshape(1, n))


def _head_rms(y, gain, bd):
    hi, lo = _split_bf16(y * y)
    ss = _dot(hi, bd) + _dot(lo, bd)
    return y * lax.rsqrt(ss * (1.0 / HEAD_DIM) + EPS) * gain


def _rope(y, cos, sin):
    n = y.shape[-1]
    lane = lax.broadcasted_iota(jnp.int32, y.shape, 1)
    first_half = (lane % (2 * ROPE_FREQS)) < ROPE_FREQS
    swapped = jnp.where(first_half, pltpu.roll(y, n - ROPE_FREQS, 1), pltpu.roll(y, ROPE_FREQS, 1))
    return y * cos + swapped * sin


def _inproj_latent_kernel(x_ref, sh_ref, sc_ref, g_ref, w_ref, cos_ref, sin_ref, qg_ref, kg_ref, bd_ref,
                          q_ref, k_ref, v_ref, l_ref, gz_ref):
    x = x_ref[0]
    h = x * _rms_scale(x) * g_ref[...] * (1.0 + sc_ref[0]) + sh_ref[0]
    p = _dot(h.astype(BF16), w_ref[...])
    cos, sin = cos_ref[...], sin_ref[...]
    bd = bd_ref[...]
    q = _head_rms(p[:, :D_ATTN], qg_ref[...], bd)
    q = _rope(q, jnp.concatenate([cos] * 4, axis=1), jnp.concatenate([sin] * 4, axis=1))
    q_ref[0] = (q * (HEAD_DIM ** -0.5)).astype(BF16)
    k = _head_rms(p[:, D_ATTN:D_ATTN + D_KV], kg_ref[...], bd[:D_KV, :D_KV])
    k_ref[0] = _rope(k, cos, sin).astype(BF16)
    v_ref[0] = p[:, D_ATTN + D_KV:D_ATTN + 2 * D_KV].astype(BF16)
    l_ref[0] = p[:, D_ATTN + 2 * D_KV:D_ATTN + 2 * D_KV + D_LRU]
    gz_ref[0] = jax.nn.gelu(p[:, D_ATTN + 2 * D_KV + D_LRU:])


def _inproj_context_kernel(x_ref, sh_ref, sc_ref, g_ref, w_ref, kg_ref, bd_ref, k_ref, v_ref, l_ref):
    x = x_ref[0]
    h = x * _rms_scale(x) * g_ref[...] * (1.0 + sc_ref[...]) + sh_ref[...]
    p = _dot(h.astype(BF16), w_ref[...])
    k_ref[0] = _head_rms(p[:, :D_KV], kg_ref[...], bd_ref[...]).astype(BF16)
    v_ref[0] = p[:, D_KV:2 * D_KV].astype(BF16)
    l_ref[0] = p[:, 2 * D_KV:]


def _full(shape):
    nd = len(shape)
    return pl.BlockSpec(shape, lambda *_: (0,) * nd)


def _inproj_latent(x, sh, sc, gpre, w_bf, cos, sin, qg, kg, bd, tm):
    B, T, D = x.shape
    tok = lambda n: pl.BlockSpec((1, tm, n), lambda b, i: (b, i, 0))
    vec = pl.BlockSpec((1, 1, D), lambda b, i: (b, 0, 0))
    rope = pl.BlockSpec((tm, LANES), lambda b, i: (i, 0))
    return pl.pallas_call(
        _inproj_latent_kernel,
        grid=(B, T // tm),
        in_specs=[tok(D), vec, vec, _full((1, D)), _full(w_bf.shape), rope, rope,
                  _full((1, D_ATTN)), _full((1, D_KV)), _full((D_ATTN, D_ATTN))],
        out_specs=[tok(D_ATTN), tok(D_KV), tok(D_KV), tok(D_LRU), tok(D_LRU)],
        out_shape=[jax.ShapeDtypeStruct((B, T, D_ATTN), BF16), jax.ShapeDtypeStruct((B, T, D_KV), BF16),
                   jax.ShapeDtypeStruct((B, T, D_KV), BF16), jax.ShapeDtypeStruct((B, T, D_LRU), F32),
                   jax.ShapeDtypeStruct((B, T, D_LRU), F32)],
        compiler_params=pltpu.CompilerParams(dimension_semantics=("arbitrary", "arbitrary"),
                                             vmem_limit_bytes=VMEM_LIMIT),
        name="inproj_latent",
    )(x, sh, sc, gpre, w_bf, cos, sin, qg, kg, bd)


def _inproj_context(ctx, sh, sc, gpre, w_bf, kg, bd, tm):
    B, C, D = ctx.shape
    tok = lambda n: pl.BlockSpec((1, tm, n), lambda b, i: (b, i, 0))
    return pl.pallas_call(
        _inproj_context_kernel,
        grid=(B, C // tm),
        in_specs=[tok(D), _full((1, D)), _full((1, D)), _full((1, D)), _full(w_bf.shape),
                  _full((1, D_KV)), _full((D_KV, D_KV))],
        out_specs=[tok(D_KV), tok(D_KV), tok(D_LRU)],
        out_shape=[jax.ShapeDtypeStruct((B, C, D_KV), BF16), jax.ShapeDtypeStruct((B, C, D_KV), BF16),
                   jax.ShapeDtypeStruct((B, C, D_LRU), F32)],
        compiler_params=pltpu.CompilerParams(dimension_semantics=("arbitrary", "arbitrary"),
                                             vmem_limit_bytes=VMEM_LIMIT),
        name="inproj_context",
    )(ctx, sh, sc, gpre, w_bf, kg, bd)


GATE_ROWS = 256
HALO = SUBLANES


def _conv_chunk(src_ref, start, n_rows, cw, cb):
    lo, hi = start - HALO, start + GATE_ROWS + HALO
    parts = []
    if lo < 0:
        parts.append(jnp.zeros((HALO, D_LRU), F32))
        lo = 0
    body_hi = min(hi, n_rows)
    parts.append(src_ref[0, lo:body_hi, :])
    if hi > n_rows:
        parts.append(jnp.zeros((HALO, D_LRU), F32))
    xh = jnp.concatenate(parts, axis=0) if len(parts) > 1 else parts[0]
    out = cb
    for j in range(CONV_W):
        off = HALO + j - CONV_PAD_LEFT
        out = out + xh[off:off + GATE_ROWS, :] * cw[j:j + 1, :]
    return out


def _tile_scan(a, b, carry, row, reverse):
    for sh in (1, 2, 4):
        if reverse:
            a_n, b_n = pltpu.roll(a, SUBLANES - sh, 0), pltpu.roll(b, SUBLANES - sh, 0)
            valid = row < SUBLANES - sh
        else:
            a_n, b_n = pltpu.roll(a, sh, 0), pltpu.roll(b, sh, 0)
            valid = row >= sh
        b = jnp.where(valid, a * b_n + b, b)
        a = jnp.where(valid, a * a_n, a)
    h = a * carry + b
    new_carry = h[0:1, :] if reverse else h[SUBLANES - 1:SUBLANES, :]
    return h, new_carry


def _rglru_kernel(lc_ref, lx_ref, gz_ref, cw_ref, cb_ref, wa_ref, wi_ref, ba_ref, bi_ref, lam_ref, o_ref,
                  a0_ref, b0_ref, a1_ref, b1_ref, *, n_ctx, n_lat):
    cw, cb = cw_ref[...], cb_ref[...]
    a_refs, b_refs = (a0_ref, a1_ref), (b0_ref, b1_ref)
    for seg_ref, seg_rows, seg_off in ((lc_ref, n_ctx, 0), (lx_ref, n_lat, n_ctx)):
        for s in range(0, seg_rows, GATE_ROWS):
            xc = _conv_chunk(seg_ref, s, seg_rows, cw, cb)
            xb = xc.astype(BF16)
            for d in range(2):
                r = jax.nn.sigmoid(_dot(xb, wa_ref[d]) + ba_ref[d])
                i = jax.nn.sigmoid(_dot(xb, wi_ref[d]) + bi_ref[d])
                log_a = -RG_C * r * jax.nn.softplus(-lam_ref[d])
                a = jnp.exp(log_a)
                th = jnp.tanh(log_a)
                b = jnp.sqrt(jnp.maximum(-2.0 * th / (1.0 - th), 0.0)) * (i * xc)
                a_refs[d][seg_off + s:seg_off + s + GATE_ROWS, :] = a
                b_refs[d][seg_off + s:seg_off + s + GATE_ROWS, :] = b

    ctx_tiles, all_tiles = n_ctx // SUBLANES, (n_ctx + n_lat) // SUBLANES
    row = lax.broadcasted_iota(jnp.int32, (SUBLANES, D_LRU), 0)

    def step(j, carry):
        cf, cr = carry
        tf = pl.multiple_of(j * SUBLANES, SUBLANES)
        jr = jnp.where(j < ctx_tiles, ctx_tiles - 1 - j, all_tiles + ctx_tiles - 1 - j)
        tr = pl.multiple_of(jr * SUBLANES, SUBLANES)
        hf, cf = _tile_scan(a0_ref[pl.ds(tf, SUBLANES), :], b0_ref[pl.ds(tf, SUBLANES), :], cf, row, False)
        hr, cr = _tile_scan(a1_ref[pl.ds(tr, SUBLANES), :], b1_ref[pl.ds(tr, SUBLANES), :], cr, row, True)
        b0_ref[pl.ds(tf, SUBLANES), :] = hf
        b1_ref[pl.ds(tr, SUBLANES), :] = hr
        return cf, cr

    zero = jnp.zeros((1, D_LRU), F32)
    lax.fori_loop(0, all_tiles, step, (zero, zero))
    for s in range(0, n_lat, GATE_ROWS):
        y = b0_ref[n_ctx + s:n_ctx + s + GATE_ROWS, :] + b1_ref[n_ctx + s:n_ctx + s + GATE_ROWS, :]
        o_ref[0, s:s + GATE_ROWS, :] = (y * gz_ref[0, s:s + GATE_ROWS, :]).astype(BF16)


def _rglru(lc, lx, gz, conv_w, conv_b, wa_bd, wi_bd, ba, bi, lam):
    B, C, _ = lc.shape
    T = lx.shape[1]
    seq = lambda n: pl.BlockSpec((1, n, D_LRU), lambda b: (b, 0, 0))
    scratch = [pltpu.VMEM((C + T, D_LRU), F32) for _ in range(4)]
    return pl.pallas_call(
        functools.partial(_rglru_kernel, n_ctx=C, n_lat=T),
        grid=(B,),
        in_specs=[seq(C), seq(T), seq(T), _full((CONV_W, D_LRU)), _full((1, D_LRU)),
                  _full((2, D_LRU, D_LRU)), _full((2, D_LRU, D_LRU)),
                  _full((2, 1, D_LRU)), _full((2, 1, D_LRU)), _full((2, 1, D_LRU))],
        out_specs=seq(T),
        out_shape=jax.ShapeDtypeStruct((B, T, D_LRU), BF16),
        scratch_shapes=scratch,
        compiler_params=pltpu.CompilerParams(dimension_semantics=("arbitrary",), vmem_limit_bytes=VMEM_LIMIT),
        name="rglru",
    )(lc, lx, gz, conv_w, conv_b, wa_bd, wi_bd, ba, bi, lam)


def _attn_kernel(q_ref, k_ref, v_ref, o_ref, *, tq):
    group = N_Q_HEADS // N_KV_HEADS
    for g in range(N_KV_HEADS):
        q = q_ref[0, g * group:(g + 1) * group].reshape(group * tq, HEAD_DIM)
        s = _dot_nt(q, k_ref[0, g])
        p = jnp.exp(s - jnp.max(s, axis=-1, keepdims=True))
        denom = jnp.sum(p, axis=-1, keepdims=True)
        o = _dot(p.astype(BF16), v_ref[0, g]) / denom
        o_ref[0, g * group:(g + 1) * group] = o.reshape(group, tq, HEAD_DIM).astype(BF16)


def _attention(q4, k4, v4, tq):
    B, H, T, _ = q4.shape
    S = k4.shape[2]
    kv = pl.BlockSpec((1, N_KV_HEADS, S, HEAD_DIM), lambda b, i: (b, 0, 0, 0))
    qo = pl.BlockSpec((1, H, tq, HEAD_DIM), lambda b, i: (b, 0, i, 0))
    return pl.pallas_call(
        functools.partial(_attn_kernel, tq=tq),
        grid=(B, T // tq),
        in_specs=[qo, kv, kv],
        out_specs=qo,
        out_shape=jax.ShapeDtypeStruct((B, H, T, HEAD_DIM), BF16),
        compiler_params=pltpu.CompilerParams(dimension_semantics=("arbitrary", "arbitrary"),
                                             vmem_limit_bytes=VMEM_LIMIT),
        name="attention",
    )(q4, k4, v4)


def _outproj_kernel(attn_ref, lru_ref, x_ref, wout_ref, gpm_ref, g1_ref, gpf_ref, sh_ref, sc_ref, wq_ref,
                    xmid_ref, hx_ref, qp_ref):
    mix = _dot(attn_ref[0], wout_ref[:D_ATTN, :]) + _dot(lru_ref[0], wout_ref[D_ATTN:, :])
    x_mid = x_ref[0] + g1_ref[0] * (mix * _rms_scale(mix) * gpm_ref[...])
    xmid_ref[0] = x_mid
    hx = (x_mid * _rms_scale(x_mid) * gpf_ref[...] * (1.0 + sc_ref[0]) + sh_ref[0]).astype(BF16)
    hx_ref[0] = hx
    qp_ref[0] = _dot(hx, wq_ref[...]).astype(BF16)


def _outproj(attn, lru, x, wout_bf, gpm, g1, gpf, sh2, sc2, wq_bf, tm):
    B, T, D = x.shape
    nq = wq_bf.shape[1]
    tok = lambda n: pl.BlockSpec((1, tm, n), lambda b, i: (b, i, 0))
    vec = pl.BlockSpec((1, 1, D), lambda b, i: (b, 0, 0))
    return pl.pallas_call(
        _outproj_kernel,
        grid=(B, T // tm),
        in_specs=[tok(D_ATTN), tok(D_LRU), tok(D), _full(wout_bf.shape), _full((1, D)), vec, _full((1, D)),
                  vec, vec, _full(wq_bf.shape)],
        out_specs=[tok(D), tok(D), tok(nq)],
        out_shape=[jax.ShapeDtypeStruct((B, T, D), F32), jax.ShapeDtypeStruct((B, T, D), BF16),
                   jax.ShapeDtypeStruct((B, T, nq), BF16)],
        compiler_params=pltpu.CompilerParams(dimension_semantics=("arbitrary", "arbitrary"),
                                             vmem_limit_bytes=VMEM_LIMIT),
        name="outproj",
    )(attn, lru, x, wout_bf, gpm, g1, gpf, sh2, sc2, wq_bf)


NEG_INF = float("-inf")
BIG = 1 << 20


def _extract_topk(vals, code, k):
    n = vals.shape[1]
    slot = lax.broadcasted_iota(jnp.int32, (k, n), 0)
    top_v = jnp.zeros((k, n), F32)
    top_c = jnp.zeros((k, n), jnp.int32)
    for j in range(k):
        m = jnp.max(vals, axis=0, keepdims=True)
        c = jnp.min(jnp.where(vals == m, code, BIG), axis=0, keepdims=True)
        top_v = jnp.where(slot == j, m, top_v)
        top_c = jnp.where(slot == j, c, top_c)
        vals = jnp.where(code == c, NEG_INF, vals)
    return top_v, top_c


def _lookup(table, pos):
    out = jnp.zeros(pos.shape, table.dtype)
    for i in range(PEER_TOPK):
        out = jnp.where(pos == i, table[i:i + 1, :], out)
    return out


def _topk_kernel(qp_ref, keys_ref, ia_ref, ib_ref, g_ref, *, tn):
    key_id = lax.broadcasted_iota(jnp.int32, (PEER_NKEYS, tn), 0)
    half = SUBLANES
    sub = lax.broadcasted_iota(jnp.int32, (half, tn), 0)
    code_groups = [sub, sub + half] + [sub + 16 * i for i in range(1, half)] + [(sub + half) * 16]
    cand_code = jnp.concatenate(code_groups, axis=0)
    for h in range(PEER_HEADS):
        tops = []
        for p in range(2):
            col = (2 * h + p) * PEER_NKEYS
            s = _dot_nt(keys_ref[2 * h + p], qp_ref[:, col:col + PEER_NKEYS])
            tops.append(_extract_topk(s, key_id, PEER_TOPK))
        (v1, i1), (v2, i2) = tops
        groups = [v1[0:1, :] + v2[:half, :], v1[0:1, :] + v2[half:, :]]
        groups += [v1[i:i + 1, :] + v2[:half, :] for i in range(1, half)]
        groups += [v1[half:, :] + v2[0:1, :]]
        top_s, top_c = _extract_topk(jnp.concatenate(groups, axis=0), cand_code, PEER_TOPK)
        e = jnp.exp(top_s - top_s[0:1, :])
        rows = slice(h * PEER_TOPK, (h + 1) * PEER_TOPK)
        g_ref[rows, :] = e / jnp.sum(e, axis=0, keepdims=True)
        ia_ref[rows, :] = _lookup(i1, top_c >> 4)
        ib_ref[rows, :] = _lookup(i2, top_c & 15)


def _peer_topk(qp, keys_bf, tn):
    n, nq = qp.shape
    out = pl.BlockSpec((N_SEL, tn), lambda i: (0, i))
    return pl.pallas_call(
        functools.partial(_topk_kernel, tn=tn),
        grid=(n // tn,),
        in_specs=[pl.BlockSpec((tn, nq), lambda i: (i, 0)), _full(keys_bf.shape)],
        out_specs=[out, out, out],
        out_shape=[jax.ShapeDtypeStruct((N_SEL, n), jnp.int32), jax.ShapeDtypeStruct((N_SEL, n), jnp.int32),
                   jax.ShapeDtypeStruct((N_SEL, n), F32)],
        compiler_params=pltpu.CompilerParams(dimension_semantics=("arbitrary",), vmem_limit_bytes=VMEM_LIMIT),
        name="peer_topk",
    )(qp, keys_bf)


EXPERT_CHUNK = 1024
GROUPS_PER_CHUNK = EXPERT_CHUNK // PEER_NKEYS
TOKEN_UNROLL = 8


def _peer_kernel(hx_ref, ia_ref, ib_ref, g_ref, ut_ref, v_ref, xmid_ref, g2_ref, gpost_ref, o_ref,
                 s_ref, *, tb, n_chunks):
    j = pl.program_id(1)
    srow = tb + SUBLANES

    def group_rows(gidx):
        return pl.ds(pl.multiple_of(gidx * srow, SUBLANES), tb)

    @pl.when(j < n_chunks)
    def _scores():
        act = jax.nn.gelu(_dot(hx_ref[...], ut_ref[0]))
        for l in range(GROUPS_PER_CHUNK):
            s_ref[group_rows(j * GROUPS_PER_CHUNK + l), :] = act[:, l * PEER_NKEYS:(l + 1) * PEER_NKEYS]

    @pl.when(j == n_chunks)
    def _select():
        sub = lax.broadcasted_iota(jnp.int32, (PEER_NKEYS, N_SEL), 0)

        def tokens(tt, carry):
            for u in range(TOKEN_UNROLL):
                t = tt * TOKEN_UNROLL + u
                first = jnp.where(sub == ia_ref[t], g_ref[t], 0.0).astype(BF16)
                second = jnp.where(sub == ib_ref[t], 1.0, 0.0).astype(BF16)
                rows = pl.ds(t, PEER_NKEYS, stride=srow)
                s_ref[rows, :] = _dot_nt(first, second) * s_ref[rows, :]
            return carry

        lax.fori_loop(0, tb // TOKEN_UNROLL, tokens, 0)

    @pl.when(j >= n_chunks)
    def _combine():
        jj = j - n_chunks
        h = jnp.concatenate([s_ref[group_rows(jj * GROUPS_PER_CHUNK + l), :] for l in range(GROUPS_PER_CHUNK)],
                            axis=1)
        part = _dot(h.astype(BF16), v_ref[...])

        @pl.when(jj == 0)
        def _():
            o_ref[...] = part

        @pl.when(jj > 0)
        def _():
            o_ref[...] += part

    @pl.when(j == 2 * n_chunks - 1)
    def _finish():
        f = o_ref[...]
        o_ref[...] = xmid_ref[...] + g2_ref[0] * (f * _rms_scale(f) * gpost_ref[...])


def _peer_dense(hx, ia3, ib3, g3, ut_bf, v_bf, xmid, g2, gpost, tb, tokens_per_batch):
    n, d = hx.shape
    n_chunks = ut_bf.shape[0]
    bpb = tokens_per_batch // tb
    tok = pl.BlockSpec((tb, d), lambda i, j: (i, 0))
    tok_once = pl.BlockSpec((tb, d), lambda i, j: (i, 0), pipeline_mode=pl.Buffered(1))
    pick = pl.BlockSpec((tb, 1, N_SEL), lambda i, j: (i, 0, 0))
    return pl.pallas_call(
        functools.partial(_peer_kernel, tb=tb, n_chunks=n_chunks),
        grid=(n // tb, 2 * n_chunks),
        in_specs=[tok_once, pick, pick, pick,
                  pl.BlockSpec((1, d, EXPERT_CHUNK), lambda i, j: (jnp.minimum(j, n_chunks - 1), 0, 0)),
                  pl.BlockSpec((EXPERT_CHUNK, d), lambda i, j: (jnp.maximum(j - n_chunks, 0), 0)),
                  tok_once, pl.BlockSpec((1, 1, d), lambda i, j: (i // bpb, 0, 0)), _full((1, d))],
        out_specs=tok,
        out_shape=jax.ShapeDtypeStruct((n, d), F32),
        scratch_shapes=[pltpu.VMEM((PEER_NKEYS * (tb + SUBLANES), PEER_NKEYS), F32)],
        compiler_params=pltpu.CompilerParams(dimension_semantics=("arbitrary", "arbitrary"),
                                             vmem_limit_bytes=VMEM_LIMIT),
        name="peer_dense",
    )(hx, ia3, ib3, g3, ut_bf, v_bf, xmid, g2, gpost)


def _rope_tables(T):
    t = np.arange(T)
    inv = ROPE_THETA ** (-jnp.arange(ROPE_FREQS, dtype=F32) / ROPE_FREQS)
    ang_r = jnp.asarray(t // GRID_W, F32)[:, None] * inv
    ang_c = jnp.asarray(t % GRID_W, F32)[:, None] * inv
    cos = jnp.concatenate([jnp.cos(ang_r)] * 2 + [jnp.cos(ang_c)] * 2, axis=1)
    sin = jnp.concatenate([-jnp.sin(ang_r), jnp.sin(ang_r), -jnp.sin(ang_c), jnp.sin(ang_c)], axis=1)
    return jnp.concatenate([cos, cos], axis=1), jnp.concatenate([sin, sin], axis=1)


def _block_diag(w):
    n, m, _ = w.shape
    eye = jnp.eye(n, dtype=w.dtype)
    return (eye[:, None, :, None] * w[:, :, None, :]).reshape(n * m, n * m)


def kernel(x, c, ctx, c_ctx, w_mod, b_mod, g_pre_mix, g_post_mix, g_pre_ffn, g_post_ffn, w_in, q_norm_g, k_norm_g, conv_w, conv_b, rg_wa, rg_ba, rg_wi, rg_bi, rg_lambda, w_out, peer_wq, peer_subkeys, peer_u, peer_v):
    B, T, D = x.shape
    C = ctx.shape[1]
    depth = w_mod.shape[0]
    cos, sin = _rope_tables(T)
    head_ones = _block_diag(jnp.ones((N_Q_HEADS, HEAD_DIM, HEAD_DIM), BF16))
    mod_rows = -(-(B + 1) // SUBLANES) * SUBLANES

    for l in range(depth):
        assert l == depth - 1, "context-stream update between layers is not implemented"
        c_all = jnp.zeros((mod_rows, D), F32).at[:B].set(c).at[B].set(c_ctx)
        mod = _modulation(c_all, w_mod[l], b_mod[l])
        mx = mod[:B].reshape(B, N_MOD, 1, D)
        sh1, sc1, g1, sh2, sc2, g2 = (mx[:, i] for i in range(N_MOD))
        mc = mod[B].reshape(N_MOD, 1, D)

        w_bf = w_in[l].astype(BF16)
        qg = jnp.tile(q_norm_g[l], N_Q_HEADS).reshape(1, D_ATTN)
        kg = jnp.tile(k_norm_g[l], N_KV_HEADS).reshape(1, D_KV)
        gpre = g_pre_mix[l].reshape(1, D)
        q, kx, vx, lx, gz = _inproj_latent(x, sh1, sc1, gpre, w_bf, cos, sin, qg, kg, head_ones, tm=512)
        kc, vc, lc = _inproj_context(ctx, mc[0], mc[1], gpre, w_bf[:, D_ATTN:D_ATTN + 2 * D_KV + D_LRU],
                                     kg, head_ones[:D_KV, :D_KV], tm=C)

        lru = _rglru(lc, lx, gz, conv_w[l], conv_b[l].reshape(1, D_LRU),
                     jnp.stack([_block_diag(rg_wa[l, d]) for d in range(2)]).astype(BF16),
                     jnp.stack([_block_diag(rg_wi[l, d]) for d in range(2)]).astype(BF16),
                     rg_ba[l].reshape(2, 1, D_LRU), rg_bi[l].reshape(2, 1, D_LRU), rg_lambda[l].reshape(2, 1, D_LRU))

        heads = lambda a, n: a.reshape(B, a.shape[1], n, HEAD_DIM).transpose(0, 2, 1, 3)
        k4 = heads(jnp.concatenate([kc, kx], axis=1), N_KV_HEADS)
        v4 = heads(jnp.concatenate([vc, vx], axis=1), N_KV_HEADS)
        attn = _attention(heads(q, N_Q_HEADS), k4, v4, tq=256).transpose(0, 2, 1, 3).reshape(B, T, D_ATTN)

        x_mid, hx, qp = _outproj(attn, lru, x, w_out[l].astype(BF16), g_post_mix[l].reshape(1, D), g1,
                                 g_pre_ffn[l].reshape(1, D), sh2, sc2, peer_wq[l].astype(BF16), tm=256)

        n = B * T
        keys_bf = peer_subkeys[l].reshape(2 * PEER_HEADS, PEER_NKEYS, -1).astype(BF16)
        ia, ib, gate = _peer_topk(qp.reshape(n, -1), keys_bf, tn=256)
        picks = lambda a: a.T.reshape(n, 1, N_SEL)
        ut = peer_u[l].astype(BF16).reshape(-1, EXPERT_CHUNK, D).transpose(0, 2, 1)
        x = _peer_dense(hx.reshape(n, D), picks(ia), picks(ib), picks(gate), ut,
                        peer_v[l].astype(BF16), x_mid.reshape(n, D), g2, g_post_ffn[l].reshape(1, D),
                        tb=min(512, T), tokens_per_batch=T).reshape(B, T, D)
    return x
```

```python
import functools

import jax
import jax.numpy as jnp
import numpy as np
from jax import lax
from jax.experimental import pallas as pl
from jax.experimental.pallas import tpu as pltpu

F32 = jnp.float32
BF16 = jnp.bfloat16

HEAD_DIM = 64
N_Q_HEADS = 8
N_KV_HEADS = 2
D_ATTN = 512
D_KV = 128
D_LRU = 512
LRU_BLOCKS = 8
CONV_W = 4
CONV_PAD_LEFT = 2
RG_C = 8.0
ROPE_THETA = 10000.0
ROPE_FREQS = 16
GRID_W = 64
N_MOD = 6
EPS = 1e-6
PEER_HEADS = 8
PEER_NKEYS = 128
PEER_TOPK = 16
N_SEL = PEER_HEADS * PEER_TOPK

SUBLANES = 8
LANES = 128
VMEM_LIMIT = 56 * 1024 * 1024

LOG2_E = 1.4426950408889634
NT_DIMS = (((1,), (1,)), ((), ()))


def _dot(a, b):
    return jnp.dot(a, b, preferred_element_type=F32)


def _dot_nt(a, b):
    return lax.dot_general(a, b, NT_DIMS, preferred_element_type=F32)


def _split_bf16(x):
    hi = x.astype(BF16)
    return hi, (x - hi.astype(F32)).astype(BF16)


def _rms_scale(x):
    return lax.rsqrt(jnp.mean(x * x, axis=-1, keepdims=True) + EPS)


def _mod_kernel(c_ref, w_ref, b_ref, o_ref):
    s = jax.nn.silu(c_ref[...])
    s_hi, s_lo = _split_bf16(s)
    w_hi, w_lo = _split_bf16(w_ref[...])
    o_ref[...] = _dot(s_hi, w_hi) + _dot(s_lo, w_hi) + _dot(s_hi, w_lo) + b_ref[...]


def _modulation(c_all, w_mod, b_mod):
    m, d = c_all.shape
    n = w_mod.shape[1]
    tn = 1536
    return pl.pallas_call(
        _mod_kernel,
        grid=(n // tn,),
        in_specs=[pl.BlockSpec((m, d), lambda j: (0, 0)),
                  pl.BlockSpec((d, tn), lambda j: (0, j)),
                  pl.BlockSpec((1, tn), lambda j: (0, j))],
        out_specs=pl.BlockSpec((m, tn), lambda j: (0, j)),
        out_shape=jax.ShapeDtypeStruct((m, n), F32),
        compiler_params=pltpu.CompilerParams(dimension_semantics=("arbitrary",), vmem_limit_bytes=VMEM_LIMIT),
        name="modulation",
    )(c_all, w_mod, b_mod.reshape(1, n))


def _head_rms(y, gain, bd):
    hi, lo = _split_bf16(y * y)
    ss = _dot(hi, bd) + _dot(lo, bd)
    return y * lax.rsqrt(ss * (1.0 / HEAD_DIM) + EPS) * gain


def _rope(y, cos, sin):
    n = y.shape[-1]
    lane = lax.broadcasted_iota(jnp.int32, y.shape, 1)
    first_half = (lane % (2 * ROPE_FREQS)) < ROPE_FREQS
    swapped = jnp.where(first_half, pltpu.roll(y, n - ROPE_FREQS, 1), pltpu.roll(y, ROPE_FREQS, 1))
    return y * cos + swapped * sin


def _inproj_latent_kernel(x_ref, sh_ref, sc_ref, g_ref, w_ref, cos_ref, sin_ref, qg_ref, kg_ref, bd_ref,
                          q_ref, k_ref, v_ref, l_ref, gz_ref):
    x = x_ref[0]
    h = x * _rms_scale(x) * g_ref[...] * (1.0 + sc_ref[0]) + sh_ref[0]
    p = _dot(h.astype(BF16), w_ref[...])
    cos, sin = cos_ref[...], sin_ref[...]
    bd = bd_ref[...]
    q = _head_rms(p[:, :D_ATTN], qg_ref[...], bd)
    q = _rope(q, jnp.concatenate([cos] * 4, axis=1), jnp.concatenate([sin] * 4, axis=1)) * (HEAD_DIM ** -0.5 * LOG2_E)
    _store_heads(q_ref, q, N_Q_HEADS)
    k = _head_rms(p[:, D_ATTN:D_ATTN + D_KV], kg_ref[...], bd[:D_KV, :D_KV])
    _store_heads(k_ref, _rope(k, cos, sin), N_KV_HEADS)
    _store_heads(v_ref, p[:, D_ATTN + D_KV:D_ATTN + 2 * D_KV], N_KV_HEADS)
    l_ref[0] = p[:, D_ATTN + 2 * D_KV:D_ATTN + 2 * D_KV + D_LRU]
    gz_ref[0] = jax.nn.gelu(p[:, D_ATTN + 2 * D_KV + D_LRU:])


def _store_heads(ref, y, n_heads):
    for h in range(n_heads):
        ref[0, h] = y[:, h * HEAD_DIM:(h + 1) * HEAD_DIM].astype(BF16)


def _inproj_context_kernel(x_ref, sh_ref, sc_ref, g_ref, w_ref, kg_ref, bd_ref, k_ref, v_ref, l_ref):
    x = x_ref[0]
    h = x * _rms_scale(x) * g_ref[...] * (1.0 + sc_ref[...]) + sh_ref[...]
    p = _dot(h.astype(BF16), w_ref[...])
    _store_heads(k_ref, _head_rms(p[:, :D_KV], kg_ref[...], bd_ref[...]), N_KV_HEADS)
    _store_heads(v_ref, p[:, D_KV:2 * D_KV], N_KV_HEADS)
    l_ref[0] = p[:, 2 * D_KV:]


def _full(shape):
    nd = len(shape)
    return pl.BlockSpec(shape, lambda *_: (0,) * nd)


def _inproj_latent(x, sh, sc, gpre, w_bf, cos, sin, qg, kg, bd, tm):
    B, T, D = x.shape
    tok = lambda n: pl.BlockSpec((1, tm, n), lambda b, i: (b, i, 0))
    vec = pl.BlockSpec((1, 1, D), lambda b, i: (b, 0, 0))
    rope = pl.BlockSpec((tm, LANES), lambda b, i: (i, 0))
    heads = lambda n: pl.BlockSpec((1, n, tm, HEAD_DIM), lambda b, i: (b, 0, i, 0))
    head_shape = lambda n: jax.ShapeDtypeStruct((B, n, T, HEAD_DIM), BF16)
    return pl.pallas_call(
        _inproj_latent_kernel,
        grid=(B, T // tm),
        in_specs=[tok(D), vec, vec, _full((1, D)), _full(w_bf.shape), rope, rope,
                  _full((1, D_ATTN)), _full((1, D_KV)), _full((D_ATTN, D_ATTN))],
        out_specs=[heads(N_Q_HEADS), heads(N_KV_HEADS), heads(N_KV_HEADS), tok(D_LRU), tok(D_LRU)],
        out_shape=[head_shape(N_Q_HEADS), head_shape(N_KV_HEADS), head_shape(N_KV_HEADS),
                   jax.ShapeDtypeStruct((B, T, D_LRU), F32), jax.ShapeDtypeStruct((B, T, D_LRU), F32)],
        compiler_params=pltpu.CompilerParams(dimension_semantics=("arbitrary", "arbitrary"),
                                             vmem_limit_bytes=VMEM_LIMIT),
        name="inproj_latent",
    )(x, sh, sc, gpre, w_bf, cos, sin, qg, kg, bd)


def _inproj_context(ctx, sh, sc, gpre, w_bf, kg, bd, tm):
    B, C, D = ctx.shape
    tok = lambda n: pl.BlockSpec((1, tm, n), lambda b, i: (b, i, 0))
    heads = pl.BlockSpec((1, N_KV_HEADS, tm, HEAD_DIM), lambda b, i: (b, 0, i, 0))
    head_shape = jax.ShapeDtypeStruct((B, N_KV_HEADS, C, HEAD_DIM), BF16)
    return pl.pallas_call(
        _inproj_context_kernel,
        grid=(B, C // tm),
        in_specs=[tok(D), _full((1, D)), _full((1, D)), _full((1, D)), _full(w_bf.shape),
                  _full((1, D_KV)), _full((D_KV, D_KV))],
        out_specs=[heads, heads, tok(D_LRU)],
        out_shape=[head_shape, head_shape, jax.ShapeDtypeStruct((B, C, D_LRU), F32)],
        compiler_params=pltpu.CompilerParams(dimension_semantics=("arbitrary", "arbitrary"),
                                             vmem_limit_bytes=VMEM_LIMIT),
        name="inproj_context",
    )(ctx, sh, sc, gpre, w_bf, kg, bd)


GATE_ROWS = 256
HALO = SUBLANES


def _conv_chunk(src_ref, start, n_rows, cw, cb):
    lo, hi = start - HALO, start + GATE_ROWS + HALO
    parts = []
    if lo < 0:
        parts.append(jnp.zeros((HALO, D_LRU), F32))
        lo = 0
    body_hi = min(hi, n_rows)
    parts.append(src_ref[0, lo:body_hi, :])
    if hi > n_rows:
        parts.append(jnp.zeros((HALO, D_LRU), F32))
    xh = jnp.concatenate(parts, axis=0) if len(parts) > 1 else parts[0]
    out = cb
    for j in range(CONV_W):
        off = HALO + j - CONV_PAD_LEFT
        out = out + xh[off:off + GATE_ROWS, :] * cw[j:j + 1, :]
    return out


def _tile_scan(a, b, carry, row, reverse):
    for sh in (1, 2, 4):
        if reverse:
            a_n, b_n = pltpu.roll(a, SUBLANES - sh, 0), pltpu.roll(b, SUBLANES - sh, 0)
            valid = row < SUBLANES - sh
        else:
            a_n, b_n = pltpu.roll(a, sh, 0), pltpu.roll(b, sh, 0)
            valid = row >= sh
        b = jnp.where(valid, a * b_n + b, b)
        a = jnp.where(valid, a * a_n, a)
    h = a * carry + b
    new_carry = h[0:1, :] if reverse else h[SUBLANES - 1:SUBLANES, :]
    return h, new_carry


def _rglru_kernel(lc_ref, lx_ref, gz_ref, cw_ref, cb_ref, wa_ref, wi_ref, ba_ref, bi_ref, lam_ref, o_ref,
                  a0_ref, b0_ref, a1_ref, b1_ref, *, n_ctx, n_lat):
    cw, cb = cw_ref[...], cb_ref[...]
    a_refs, b_refs = (a0_ref, a1_ref), (b0_ref, b1_ref)
    for seg_ref, seg_rows, seg_off in ((lc_ref, n_ctx, 0), (lx_ref, n_lat, n_ctx)):
        for s in range(0, seg_rows, GATE_ROWS):
            xc = _conv_chunk(seg_ref, s, seg_rows, cw, cb)
            xb = xc.astype(BF16)
            for d in range(2):
                r = jax.nn.sigmoid(_dot(xb, wa_ref[d]) + ba_ref[d])
                i = jax.nn.sigmoid(_dot(xb, wi_ref[d]) + bi_ref[d])
                log_a = -RG_C * r * jax.nn.softplus(-lam_ref[d])
                a = jnp.exp(log_a)
                b = jnp.sqrt(jnp.maximum(1.0 - a * a, 0.0)) * (i * xc)
                a_refs[d][seg_off + s:seg_off + s + GATE_ROWS, :] = a
                b_refs[d][seg_off + s:seg_off + s + GATE_ROWS, :] = b

    ctx_tiles, all_tiles = n_ctx // SUBLANES, (n_ctx + n_lat) // SUBLANES
    row = lax.broadcasted_iota(jnp.int32, (SUBLANES, D_LRU), 0)

    def step(j, carry):
        cf, cr = carry
        tf = pl.multiple_of(j * SUBLANES, SUBLANES)
        jr = jnp.where(j < ctx_tiles, ctx_tiles - 1 - j, all_tiles + ctx_tiles - 1 - j)
        tr = pl.multiple_of(jr * SUBLANES, SUBLANES)
        hf, cf = _tile_scan(a0_ref[pl.ds(tf, SUBLANES), :], b0_ref[pl.ds(tf, SUBLANES), :], cf, row, False)
        hr, cr = _tile_scan(a1_ref[pl.ds(tr, SUBLANES), :], b1_ref[pl.ds(tr, SUBLANES), :], cr, row, True)
        b0_ref[pl.ds(tf, SUBLANES), :] = hf
        b1_ref[pl.ds(tr, SUBLANES), :] = hr
        return cf, cr

    zero = jnp.zeros((1, D_LRU), F32)
    lax.fori_loop(0, all_tiles, step, (zero, zero))
    for s in range(0, n_lat, GATE_ROWS):
        y = b0_ref[n_ctx + s:n_ctx + s + GATE_ROWS, :] + b1_ref[n_ctx + s:n_ctx + s + GATE_ROWS, :]
        o_ref[0, s:s + GATE_ROWS, :] = (y * gz_ref[0, s:s + GATE_ROWS, :]).astype(BF16)


def _rglru(lc, lx, gz, conv_w, conv_b, wa_bd, wi_bd, ba, bi, lam):
    B, C, _ = lc.shape
    T = lx.shape[1]
    seq = lambda n: pl.BlockSpec((1, n, D_LRU), lambda b: (b, 0, 0))
    scratch = [pltpu.VMEM((C + T, D_LRU), F32) for _ in range(4)]
    return pl.pallas_call(
        functools.partial(_rglru_kernel, n_ctx=C, n_lat=T),
        grid=(B,),
        in_specs=[seq(C), seq(T), seq(T), _full((CONV_W, D_LRU)), _full((1, D_LRU)),
                  _full((2, D_LRU, D_LRU)), _full((2, D_LRU, D_LRU)),
                  _full((2, 1, D_LRU)), _full((2, 1, D_LRU)), _full((2, 1, D_LRU))],
        out_specs=seq(T),
        out_shape=jax.ShapeDtypeStruct((B, T, D_LRU), BF16),
        scratch_shapes=scratch,
        compiler_params=pltpu.CompilerParams(dimension_semantics=("arbitrary",), vmem_limit_bytes=VMEM_LIMIT),
        name="rglru",
    )(lc, lx, gz, conv_w, conv_b, wa_bd, wi_bd, ba, bi, lam)


def _attn_kernel(q_ref, k_ref, v_ref, o_ref, *, tq):
    group = N_Q_HEADS // N_KV_HEADS
    for pair in range(N_Q_HEADS // 2):
        g = 2 * pair // group
        q = q_ref[0, 2 * pair:2 * pair + 2].reshape(2 * tq, HEAD_DIM)
        s = _dot_nt(q, k_ref[0, g])
        p = jnp.exp2(s - jnp.max(s, axis=-1, keepdims=True))
        denom = jnp.sum(p, axis=-1, keepdims=True)
        o = _dot(p.astype(BF16), v_ref[0, g]) / denom
        o_ref[0, :, 2 * pair * HEAD_DIM:(2 * pair + 2) * HEAD_DIM] = (
            jnp.concatenate([o[:tq], o[tq:]], axis=1).astype(BF16))


def _attention(q4, k4, v4, tq):
    B, H, T, _ = q4.shape
    S = k4.shape[2]
    kv = pl.BlockSpec((1, N_KV_HEADS, S, HEAD_DIM), lambda b, i: (b, 0, 0, 0))
    return pl.pallas_call(
        functools.partial(_attn_kernel, tq=tq),
        grid=(B, T // tq),
        in_specs=[pl.BlockSpec((1, H, tq, HEAD_DIM), lambda b, i: (b, 0, i, 0)), kv, kv],
        out_specs=pl.BlockSpec((1, tq, H * HEAD_DIM), lambda b, i: (b, i, 0)),
        out_shape=jax.ShapeDtypeStruct((B, T, H * HEAD_DIM), BF16),
        compiler_params=pltpu.CompilerParams(dimension_semantics=("arbitrary", "arbitrary"),
                                             vmem_limit_bytes=VMEM_LIMIT),
        name="attention",
    )(q4, k4, v4)


def _outproj_kernel(attn_ref, lru_ref, x_ref, wout_ref, gpm_ref, g1_ref, gpf_ref, sh_ref, sc_ref, wq_ref,
                    xmid_ref, hx_ref, qp_ref):
    mix = _dot(attn_ref[0], wout_ref[:D_ATTN, :]) + _dot(lru_ref[0], wout_ref[D_ATTN:, :])
    x_mid = x_ref[0] + g1_ref[0] * (mix * _rms_scale(mix) * gpm_ref[...])
    xmid_ref[0] = x_mid
    hx = (x_mid * _rms_scale(x_mid) * gpf_ref[...] * (1.0 + sc_ref[0]) + sh_ref[0]).astype(BF16)
    hx_ref[0] = hx
    qp_ref[0] = _dot(hx, wq_ref[...]).astype(BF16)


def _outproj(attn, lru, x, wout_bf, gpm, g1, gpf, sh2, sc2, wq_bf, tm):
    B, T, D = x.shape
    nq = wq_bf.shape[1]
    tok = lambda n: pl.BlockSpec((1, tm, n), lambda b, i: (b, i, 0))
    vec = pl.BlockSpec((1, 1, D), lambda b, i: (b, 0, 0))
    return pl.pallas_call(
        _outproj_kernel,
        grid=(B, T // tm),
        in_specs=[tok(D_ATTN), tok(D_LRU), tok(D), _full(wout_bf.shape), _full((1, D)), vec, _full((1, D)),
                  vec, vec, _full(wq_bf.shape)],
        out_specs=[tok(D), tok(D), tok(nq)],
        out_shape=[jax.ShapeDtypeStruct((B, T, D), F32), jax.ShapeDtypeStruct((B, T, D), BF16),
                   jax.ShapeDtypeStruct((B, T, nq), BF16)],
        compiler_params=pltpu.CompilerParams(dimension_semantics=("arbitrary", "arbitrary"),
                                             vmem_limit_bytes=VMEM_LIMIT),
        name="outproj",
    )(attn, lru, x, wout_bf, gpm, g1, gpf, sh2, sc2, wq_bf)


NEG_INF = float("-inf")
BIG = 1e9


def _slab_argmax(vs, cs):
    while len(vs) > 1:
        nv, nc = [], []
        for i in range(0, len(vs) - 1, 2):
            take = vs[i + 1] > vs[i]
            nv.append(jnp.where(take, vs[i + 1], vs[i]))
            nc.append(jnp.where(take, cs[i + 1], cs[i]))
        if len(vs) % 2:
            nv.append(vs[-1])
            nc.append(cs[-1])
        vs, cs = nv, nc
    return vs[0], cs[0]


def _extract_topk(vs, cs, k):
    n = vs[0].shape[1]
    slot = lax.broadcasted_iota(jnp.int32, (k, n), 0)
    top_v = jnp.zeros((k, n), F32)
    top_c = jnp.zeros((k, n), F32)
    for j in range(k):
        v, c = _slab_argmax(vs, cs)
        m = jnp.max(v, axis=0, keepdims=True)
        cm = jnp.min(jnp.where(v == m, c, BIG), axis=0, keepdims=True)
        top_v = jnp.where(slot == j, m, top_v)
        top_c = jnp.where(slot == j, cm, top_c)
        vs = [jnp.where(ci == cm, NEG_INF, vi) for vi, ci in zip(vs, cs)]
    return top_v, top_c


def _lookup(table, pos):
    out = jnp.zeros(pos.shape, table.dtype)
    for i in range(PEER_TOPK):
        out = jnp.where(pos == i, table[i:i + 1, :], out)
    return out


def _topk_kernel(qp_ref, keys_ref, ia_ref, ib_ref, g_ref, *, tn):
    half = SUBLANES
    sub = lax.broadcasted_iota(jnp.int32, (half, tn), 0).astype(F32)
    key_codes = [sub + float(half * s) for s in range(PEER_NKEYS // half)]
    cand_codes = [sub, sub + float(half)] + [sub + 16.0 * i for i in range(1, half)] + [(sub + float(half)) * 16.0]
    for h in range(PEER_HEADS):
        tops = []
        for p in range(2):
            col = (2 * h + p) * PEER_NKEYS
            s = _dot_nt(keys_ref[2 * h + p], qp_ref[:, col:col + PEER_NKEYS])
            tops.append(_extract_topk([s[half * r:half * (r + 1), :] for r in range(PEER_NKEYS // half)],
                                      key_codes, PEER_TOPK))
        (v1, i1), (v2, i2) = tops
        groups = [v1[0:1, :] + v2[:half, :], v1[0:1, :] + v2[half:, :]]
        groups += [v1[i:i + 1, :] + v2[:half, :] for i in range(1, half)]
        groups += [v1[half:, :] + v2[0:1, :]]
        top_s, top_c = _extract_topk(groups, cand_codes, PEER_TOPK)
        e = jnp.exp(top_s - top_s[0:1, :])
        rows = slice(h * PEER_TOPK, (h + 1) * PEER_TOPK)
        g_ref[rows, :] = e / jnp.sum(e, axis=0, keepdims=True)
        pair = top_c.astype(jnp.int32)
        ia_ref[rows, :] = _lookup(i1, pair >> 4).astype(jnp.int32)
        ib_ref[rows, :] = _lookup(i2, pair & 15).astype(jnp.int32)


def _peer_topk(qp, keys_bf, tn):
    n, nq = qp.shape
    out = pl.BlockSpec((N_SEL, tn), lambda i: (0, i))
    return pl.pallas_call(
        functools.partial(_topk_kernel, tn=tn),
        grid=(n // tn,),
        in_specs=[pl.BlockSpec((tn, nq), lambda i: (i, 0)), _full(keys_bf.shape)],
        out_specs=[out, out, out],
        out_shape=[jax.ShapeDtypeStruct((N_SEL, n), jnp.int32), jax.ShapeDtypeStruct((N_SEL, n), jnp.int32),
                   jax.ShapeDtypeStruct((N_SEL, n), F32)],
        compiler_params=pltpu.CompilerParams(dimension_semantics=("arbitrary",), vmem_limit_bytes=VMEM_LIMIT),
        name="peer_topk",
    )(qp, keys_bf)


EXPERT_CHUNK = 1024
GROUPS_PER_CHUNK = EXPERT_CHUNK // PEER_NKEYS
TOKEN_UNROLL = 8


def _peer_kernel(hx_ref, ia_ref, ib_ref, g_ref, ut_ref, v_ref, xmid_ref, g2_ref, gpost_ref, o_ref,
                 s_ref, *, tb, n_chunks):
    j = pl.program_id(1)
    srow = tb + SUBLANES

    def group_rows(gidx):
        return pl.ds(pl.multiple_of(gidx * srow, SUBLANES), tb)

    @pl.when(j < n_chunks)
    def _scores():
        act = jax.nn.gelu(_dot(hx_ref[...], ut_ref[0]))
        for l in range(GROUPS_PER_CHUNK):
            s_ref[group_rows(j * GROUPS_PER_CHUNK + l), :] = act[:, l * PEER_NKEYS:(l + 1) * PEER_NKEYS]

    @pl.when(j == n_chunks)
    def _select():
        sub = lax.broadcasted_iota(jnp.int32, (PEER_NKEYS, N_SEL), 0)

        def tokens(tt, carry):
            for u in range(TOKEN_UNROLL):
                t = tt * TOKEN_UNROLL + u
                first = jnp.where(sub == ia_ref[t], g_ref[t], 0.0).astype(BF16)
                second = jnp.where(sub == ib_ref[t], 1.0, 0.0).astype(BF16)
                rows = pl.ds(t, PEER_NKEYS, stride=srow)
                s_ref[rows, :] = _dot_nt(first, second) * s_ref[rows, :]
            return carry

        lax.fori_loop(0, tb // TOKEN_UNROLL, tokens, 0)

    @pl.when(j >= n_chunks)
    def _combine():
        jj = j - n_chunks
        h = jnp.concatenate([s_ref[group_rows(jj * GROUPS_PER_CHUNK + l), :] for l in range(GROUPS_PER_CHUNK)],
                            axis=1)
        part = _dot(h.astype(BF16), v_ref[...])

        @pl.when(jj == 0)
        def _():
            o_ref[...] = part

        @pl.when(jj > 0)
        def _():
            o_ref[...] += part

    @pl.when(j == 2 * n_chunks - 1)
    def _finish():
        f = o_ref[...]
        o_ref[...] = xmid_ref[...] + g2_ref[0] * (f * _rms_scale(f) * gpost_ref[...])


def _peer_dense(hx, ia3, ib3, g3, ut_bf, v_bf, xmid, g2, gpost, tb, tokens_per_batch):
    n, d = hx.shape
    n_chunks = ut_bf.shape[0]
    bpb = tokens_per_batch // tb
    tok = pl.BlockSpec((tb, d), lambda i, j: (i, 0))
    tok_once = pl.BlockSpec((tb, d), lambda i, j: (i, 0), pipeline_mode=pl.Buffered(1))
    pick = pl.BlockSpec((tb, 1, N_SEL), lambda i, j: (i, 0, 0))
    return pl.pallas_call(
        functools.partial(_peer_kernel, tb=tb, n_chunks=n_chunks),
        grid=(n // tb, 2 * n_chunks),
        in_specs=[tok_once, pick, pick, pick,
                  pl.BlockSpec((1, d, EXPERT_CHUNK), lambda i, j: (jnp.minimum(j, n_chunks - 1), 0, 0)),
                  pl.BlockSpec((EXPERT_CHUNK, d), lambda i, j: (jnp.maximum(j - n_chunks, 0), 0)),
                  tok_once, pl.BlockSpec((1, 1, d), lambda i, j: (i // bpb, 0, 0)), _full((1, d))],
        out_specs=tok,
        out_shape=jax.ShapeDtypeStruct((n, d), F32),
        scratch_shapes=[pltpu.VMEM((PEER_NKEYS * (tb + SUBLANES), PEER_NKEYS), F32)],
        compiler_params=pltpu.CompilerParams(dimension_semantics=("arbitrary", "arbitrary"),
                                             vmem_limit_bytes=VMEM_LIMIT),
        name="peer_dense",
    )(hx, ia3, ib3, g3, ut_bf, v_bf, xmid, g2, gpost)


def _rope_tables(T):
    t = np.arange(T)
    inv = ROPE_THETA ** (-jnp.arange(ROPE_FREQS, dtype=F32) / ROPE_FREQS)
    ang_r = jnp.asarray(t // GRID_W, F32)[:, None] * inv
    ang_c = jnp.asarray(t % GRID_W, F32)[:, None] * inv
    cos = jnp.concatenate([jnp.cos(ang_r)] * 2 + [jnp.cos(ang_c)] * 2, axis=1)
    sin = jnp.concatenate([-jnp.sin(ang_r), jnp.sin(ang_r), -jnp.sin(ang_c), jnp.sin(ang_c)], axis=1)
    return jnp.concatenate([cos, cos], axis=1), jnp.concatenate([sin, sin], axis=1)


def _block_diag(w):
    n, m, _ = w.shape
    eye = jnp.eye(n, dtype=w.dtype)
    return (eye[:, None, :, None] * w[:, :, None, :]).reshape(n * m, n * m)


def kernel(x, c, ctx, c_ctx, w_mod, b_mod, g_pre_mix, g_post_mix, g_pre_ffn, g_post_ffn, w_in, q_norm_g, k_norm_g, conv_w, conv_b, rg_wa, rg_ba, rg_wi, rg_bi, rg_lambda, w_out, peer_wq, peer_subkeys, peer_u, peer_v):
    B, T, D = x.shape
    C = ctx.shape[1]
    depth = w_mod.shape[0]
    cos, sin = _rope_tables(T)
    head_ones = _block_diag(jnp.ones((N_Q_HEADS, HEAD_DIM, HEAD_DIM), BF16))
    mod_rows = -(-(B + 1) // SUBLANES) * SUBLANES

    for l in range(depth):
        assert l == depth - 1, "context-stream update between layers is not implemented"
        c_all = jnp.zeros((mod_rows, D), F32).at[:B].set(c).at[B].set(c_ctx)
        mod = _modulation(c_all, w_mod[l], b_mod[l])
        mx = mod[:B].reshape(B, N_MOD, 1, D)
        sh1, sc1, g1, sh2, sc2, g2 = (mx[:, i] for i in range(N_MOD))
        mc = mod[B].reshape(N_MOD, 1, D)

        w_bf = w_in[l].astype(BF16)
        qg = jnp.tile(q_norm_g[l], N_Q_HEADS).reshape(1, D_ATTN)
        kg = jnp.tile(k_norm_g[l], N_KV_HEADS).reshape(1, D_KV)
        gpre = g_pre_mix[l].reshape(1, D)
        q, kx, vx, lx, gz = _inproj_latent(x, sh1, sc1, gpre, w_bf, cos, sin, qg, kg, head_ones, tm=512)
        kc, vc, lc = _inproj_context(ctx, mc[0], mc[1], gpre, w_bf[:, D_ATTN:D_ATTN + 2 * D_KV + D_LRU],
                                     kg, head_ones[:D_KV, :D_KV], tm=C)

        lru = _rglru(lc, lx, gz, conv_w[l], conv_b[l].reshape(1, D_LRU),
                     jnp.stack([_block_diag(rg_wa[l, d]) for d in range(2)]).astype(BF16),
                     jnp.stack([_block_diag(rg_wi[l, d]) for d in range(2)]).astype(BF16),
                     rg_ba[l].reshape(2, 1, D_LRU), rg_bi[l].reshape(2, 1, D_LRU), rg_lambda[l].reshape(2, 1, D_LRU))

        attn = _attention(q, jnp.concatenate([kc, kx], axis=2), jnp.concatenate([vc, vx], axis=2), tq=256)

        x_mid, hx, qp = _outproj(attn, lru, x, w_out[l].astype(BF16), g_post_mix[l].reshape(1, D), g1,
                                 g_pre_ffn[l].reshape(1, D), sh2, sc2, peer_wq[l].astype(BF16), tm=256)

        n = B * T
        keys_bf = peer_subkeys[l].reshape(2 * PEER_HEADS, PEER_NKEYS, -1).astype(BF16)
        ia, ib, gate = _peer_topk(qp.reshape(n, -1), keys_bf, tn=256)
        picks = lambda a: a.T.reshape(n, 1, N_SEL)
        ut = peer_u[l].astype(BF16).reshape(-1, EXPERT_CHUNK, D).transpose(0, 2, 1)
        x = _peer_dense(hx.reshape(n, D), picks(ia), picks(ib), picks(gate), ut,
                        peer_v[l].astype(BF16), x_mid.reshape(n, D), g2, g_post_ffn[l].reshape(1, D),
                        tb=min(512, T), tokens_per_batch=T).reshape(B, T, D)
    return x
```

```python
import functools

import jax
import jax.numpy as jnp
import numpy as np
from jax import lax
from jax.experimental import pallas as pl
from jax.experimental.pallas import tpu as pltpu

F32 = jnp.float32
BF16 = jnp.bfloat16

HEAD_DIM = 64
N_Q_HEADS = 8
N_KV_HEADS = 2
D_ATTN = 512
D_KV = 128
D_LRU = 512
LRU_BLOCKS = 8
CONV_W = 4
CONV_PAD_LEFT = 2
RG_C = 8.0
ROPE_THETA = 10000.0
ROPE_FREQS = 16
GRID_W = 64
N_MOD = 6
EPS = 1e-6
PEER_HEADS = 8
PEER_NKEYS = 128
PEER_TOPK = 16
N_SEL = PEER_HEADS * PEER_TOPK

SUBLANES = 8
LANES = 128
VMEM_LIMIT = 56 * 1024 * 1024

LOG2_E = 1.4426950408889634
NT_DIMS = (((1,), (1,)), ((), ()))


def _dot(a, b):
    return jnp.dot(a, b, preferred_element_type=F32)


def _dot_nt(a, b):
    return lax.dot_general(a, b, NT_DIMS, preferred_element_type=F32)


def _split_bf16(x):
    hi = x.astype(BF16)
    return hi, (x - hi.astype(F32)).astype(BF16)


def _rms_scale(x):
    return lax.rsqrt(jnp.mean(x * x, axis=-1, keepdims=True) + EPS)


def _mod_kernel(c_ref, w_ref, b_ref, o_ref):
    s = jax.nn.silu(c_ref[...])
    s_hi, s_lo = _split_bf16(s)
    w_hi, w_lo = _split_bf16(w_ref[...])
    o_ref[...] = _dot(s_hi, w_hi) + _dot(s_lo, w_hi) + _dot(s_hi, w_lo) + b_ref[...]


def _modulation(c_all, w_mod, b_mod):
    m, d = c_all.shape
    n = w_mod.shape[1]
    tn = 1536
    return pl.pallas_call(
        _mod_kernel,
        grid=(n // tn,),
        in_specs=[pl.BlockSpec((m, d), lambda j: (0, 0)),
                  pl.BlockSpec((d, tn), lambda j: (0, j)),
                  pl.BlockSpec((1, tn), lambda j: (0, j))],
        out_specs=pl.BlockSpec((m, tn), lambda j: (0, j)),
        out_shape=jax.ShapeDtypeStruct((m, n), F32),
        compiler_params=pltpu.CompilerParams(dimension_semantics=("arbitrary",), vmem_limit_bytes=VMEM_LIMIT),
        name="modulation",
    )(c_all, w_mod, b_mod.reshape(1, n))


def _head_rms(y, gain, bd):
    hi, lo = _split_bf16(y * y)
    ss = _dot(hi, bd) + _dot(lo, bd)
    return y * lax.rsqrt(ss * (1.0 / HEAD_DIM) + EPS) * gain


def _rope(y, cos, sin):
    n = y.shape[-1]
    lane = lax.broadcasted_iota(jnp.int32, y.shape, 1)
    first_half = (lane % (2 * ROPE_FREQS)) < ROPE_FREQS
    swapped = jnp.where(first_half, pltpu.roll(y, n - ROPE_FREQS, 1), pltpu.roll(y, ROPE_FREQS, 1))
    return y * cos + swapped * sin


def _inproj_latent_kernel(x_ref, sh_ref, sc_ref, g_ref, w_ref, cos_ref, sin_ref, qg_ref, kg_ref, bd_ref,
                          q_ref, k_ref, v_ref, l_ref, gz_ref):
    x = x_ref[0]
    h = x * _rms_scale(x) * g_ref[...] * (1.0 + sc_ref[0]) + sh_ref[0]
    p = _dot(h.astype(BF16), w_ref[...])
    cos, sin = cos_ref[...], sin_ref[...]
    bd = bd_ref[...]
    q = _head_rms(p[:, :D_ATTN], qg_ref[...], bd)
    q = _rope(q, jnp.concatenate([cos] * 4, axis=1), jnp.concatenate([sin] * 4, axis=1)) * (HEAD_DIM ** -0.5 * LOG2_E)
    _store_heads(q_ref, q, N_Q_HEADS)
    k = _head_rms(p[:, D_ATTN:D_ATTN + D_KV], kg_ref[...], bd[:D_KV, :D_KV])
    _store_heads(k_ref, _rope(k, cos, sin), N_KV_HEADS)
    _store_heads(v_ref, p[:, D_ATTN + D_KV:D_ATTN + 2 * D_KV], N_KV_HEADS)
    l_ref[0] = p[:, D_ATTN + 2 * D_KV:D_ATTN + 2 * D_KV + D_LRU]
    gz_ref[0] = jax.nn.gelu(p[:, D_ATTN + 2 * D_KV + D_LRU:])


def _store_heads(ref, y, n_heads):
    for h in range(n_heads):
        ref[0, h] = y[:, h * HEAD_DIM:(h + 1) * HEAD_DIM].astype(BF16)


def _inproj_context_kernel(x_ref, sh_ref, sc_ref, g_ref, w_ref, kg_ref, bd_ref, k_ref, v_ref, l_ref):
    x = x_ref[0]
    h = x * _rms_scale(x) * g_ref[...] * (1.0 + sc_ref[...]) + sh_ref[...]
    p = _dot(h.astype(BF16), w_ref[...])
    _store_heads(k_ref, _head_rms(p[:, :D_KV], kg_ref[...], bd_ref[...]), N_KV_HEADS)
    _store_heads(v_ref, p[:, D_KV:2 * D_KV], N_KV_HEADS)
    l_ref[0] = p[:, 2 * D_KV:]


def _full(shape):
    nd = len(shape)
    return pl.BlockSpec(shape, lambda *_: (0,) * nd)


def _inproj_latent(x, sh, sc, gpre, w_bf, cos, sin, qg, kg, bd, tm):
    B, T, D = x.shape
    tok = lambda n: pl.BlockSpec((1, tm, n), lambda b, i: (b, i, 0))
    vec = pl.BlockSpec((1, 1, D), lambda b, i: (b, 0, 0))
    rope = pl.BlockSpec((tm, LANES), lambda b, i: (i, 0))
    heads = lambda n: pl.BlockSpec((1, n, tm, HEAD_DIM), lambda b, i: (b, 0, i, 0))
    head_shape = lambda n: jax.ShapeDtypeStruct((B, n, T, HEAD_DIM), BF16)
    return pl.pallas_call(
        _inproj_latent_kernel,
        grid=(B, T // tm),
        in_specs=[tok(D), vec, vec, _full((1, D)), _full(w_bf.shape), rope, rope,
                  _full((1, D_ATTN)), _full((1, D_KV)), _full((D_ATTN, D_ATTN))],
        out_specs=[heads(N_Q_HEADS), heads(N_KV_HEADS), heads(N_KV_HEADS), tok(D_LRU), tok(D_LRU)],
        out_shape=[head_shape(N_Q_HEADS), head_shape(N_KV_HEADS), head_shape(N_KV_HEADS),
                   jax.ShapeDtypeStruct((B, T, D_LRU), F32), jax.ShapeDtypeStruct((B, T, D_LRU), F32)],
        compiler_params=pltpu.CompilerParams(dimension_semantics=("arbitrary", "arbitrary"),
                                             vmem_limit_bytes=VMEM_LIMIT),
        name="inproj_latent",
    )(x, sh, sc, gpre, w_bf, cos, sin, qg, kg, bd)


def _inproj_context(ctx, sh, sc, gpre, w_bf, kg, bd, tm):
    B, C, D = ctx.shape
    tok = lambda n: pl.BlockSpec((1, tm, n), lambda b, i: (b, i, 0))
    heads = pl.BlockSpec((1, N_KV_HEADS, tm, HEAD_DIM), lambda b, i: (b, 0, i, 0))
    head_shape = jax.ShapeDtypeStruct((B, N_KV_HEADS, C, HEAD_DIM), BF16)
    return pl.pallas_call(
        _inproj_context_kernel,
        grid=(B, C // tm),
        in_specs=[tok(D), _full((1, D)), _full((1, D)), _full((1, D)), _full(w_bf.shape),
                  _full((1, D_KV)), _full((D_KV, D_KV))],
        out_specs=[heads, heads, tok(D_LRU)],
        out_shape=[head_shape, head_shape, jax.ShapeDtypeStruct((B, C, D_LRU), F32)],
        compiler_params=pltpu.CompilerParams(dimension_semantics=("arbitrary", "arbitrary"),
                                             vmem_limit_bytes=VMEM_LIMIT),
        name="inproj_context",
    )(ctx, sh, sc, gpre, w_bf, kg, bd)


GATE_ROWS = 256
HALO = SUBLANES


def _conv_chunk(src_ref, start, n_rows, cw, cb):
    lo, hi = start - HALO, start + GATE_ROWS + HALO
    parts = []
    if lo < 0:
        parts.append(jnp.zeros((HALO, D_LRU), F32))
        lo = 0
    body_hi = min(hi, n_rows)
    parts.append(src_ref[0, lo:body_hi, :])
    if hi > n_rows:
        parts.append(jnp.zeros((HALO, D_LRU), F32))
    xh = jnp.concatenate(parts, axis=0) if len(parts) > 1 else parts[0]
    out = cb
    for j in range(CONV_W):
        off = HALO + j - CONV_PAD_LEFT
        out = out + xh[off:off + GATE_ROWS, :] * cw[j:j + 1, :]
    return out


def _tile_scan(a, b, carry, row, reverse):
    for sh in (1, 2, 4):
        if reverse:
            a_n, b_n = pltpu.roll(a, SUBLANES - sh, 0), pltpu.roll(b, SUBLANES - sh, 0)
            valid = row < SUBLANES - sh
        else:
            a_n, b_n = pltpu.roll(a, sh, 0), pltpu.roll(b, sh, 0)
            valid = row >= sh
        b = jnp.where(valid, a * b_n + b, b)
        a = jnp.where(valid, a * a_n, a)
    h = a * carry + b
    new_carry = h[0:1, :] if reverse else h[SUBLANES - 1:SUBLANES, :]
    return h, new_carry


def _rglru_kernel(lc_ref, lx_ref, gz_ref, cw_ref, cb_ref, wa_ref, wi_ref, ba_ref, bi_ref, lam_ref, o_ref,
                  a0_ref, b0_ref, a1_ref, b1_ref, *, n_ctx, n_lat):
    cw, cb = cw_ref[...], cb_ref[...]
    a_refs, b_refs = (a0_ref, a1_ref), (b0_ref, b1_ref)
    for seg_ref, seg_rows, seg_off in ((lc_ref, n_ctx, 0), (lx_ref, n_lat, n_ctx)):
        for s in range(0, seg_rows, GATE_ROWS):
            xc = _conv_chunk(seg_ref, s, seg_rows, cw, cb)
            xb = xc.astype(BF16)
            for d in range(2):
                r = jax.nn.sigmoid(_dot(xb, wa_ref[d]) + ba_ref[d])
                i = jax.nn.sigmoid(_dot(xb, wi_ref[d]) + bi_ref[d])
                log_a = -RG_C * r * jax.nn.softplus(-lam_ref[d])
                a = jnp.exp(log_a)
                b = jnp.sqrt(jnp.maximum(1.0 - a * a, 0.0)) * (i * xc)
                a_refs[d][seg_off + s:seg_off + s + GATE_ROWS, :] = a
                b_refs[d][seg_off + s:seg_off + s + GATE_ROWS, :] = b

    ctx_tiles, all_tiles = n_ctx // SUBLANES, (n_ctx + n_lat) // SUBLANES
    row = lax.broadcasted_iota(jnp.int32, (SUBLANES, D_LRU), 0)

    def step(j, carry):
        cf, cr = carry
        tf = pl.multiple_of(j * SUBLANES, SUBLANES)
        jr = jnp.where(j < ctx_tiles, ctx_tiles - 1 - j, all_tiles + ctx_tiles - 1 - j)
        tr = pl.multiple_of(jr * SUBLANES, SUBLANES)
        hf, cf = _tile_scan(a0_ref[pl.ds(tf, SUBLANES), :], b0_ref[pl.ds(tf, SUBLANES), :], cf, row, False)
        hr, cr = _tile_scan(a1_ref[pl.ds(tr, SUBLANES), :], b1_ref[pl.ds(tr, SUBLANES), :], cr, row, True)
        b0_ref[pl.ds(tf, SUBLANES), :] = hf
        b1_ref[pl.ds(tr, SUBLANES), :] = hr
        return cf, cr

    zero = jnp.zeros((1, D_LRU), F32)
    lax.fori_loop(0, all_tiles, step, (zero, zero))
    for s in range(0, n_lat, GATE_ROWS):
        y = b0_ref[n_ctx + s:n_ctx + s + GATE_ROWS, :] + b1_ref[n_ctx + s:n_ctx + s + GATE_ROWS, :]
        o_ref[0, s:s + GATE_ROWS, :] = (y * gz_ref[0, s:s + GATE_ROWS, :]).astype(BF16)


def _rglru(lc, lx, gz, conv_w, conv_b, wa_bd, wi_bd, ba, bi, lam):
    B, C, _ = lc.shape
    T = lx.shape[1]
    seq = lambda n: pl.BlockSpec((1, n, D_LRU), lambda b: (b, 0, 0))
    scratch = [pltpu.VMEM((C + T, D_LRU), F32) for _ in range(4)]
    return pl.pallas_call(
        functools.partial(_rglru_kernel, n_ctx=C, n_lat=T),
        grid=(B,),
        in_specs=[seq(C), seq(T), seq(T), _full((CONV_W, D_LRU)), _full((1, D_LRU)),
                  _full((2, D_LRU, D_LRU)), _full((2, D_LRU, D_LRU)),
                  _full((2, 1, D_LRU)), _full((2, 1, D_LRU)), _full((2, 1, D_LRU))],
        out_specs=seq(T),
        out_shape=jax.ShapeDtypeStruct((B, T, D_LRU), BF16),
        scratch_shapes=scratch,
        compiler_params=pltpu.CompilerParams(dimension_semantics=("arbitrary",), vmem_limit_bytes=VMEM_LIMIT),
        name="rglru",
    )(lc, lx, gz, conv_w, conv_b, wa_bd, wi_bd, ba, bi, lam)


def _attn_kernel(q_ref, k_ref, v_ref, o_ref, *, tq):
    group = N_Q_HEADS // N_KV_HEADS
    for pair in range(N_Q_HEADS // 2):
        g = 2 * pair // group
        q = q_ref[0, 2 * pair:2 * pair + 2].reshape(2 * tq, HEAD_DIM)
        s = _dot_nt(q, k_ref[0, g])
        p = jnp.exp2(s - jnp.max(s, axis=-1, keepdims=True))
        denom = jnp.sum(p, axis=-1, keepdims=True)
        o = _dot(p.astype(BF16), v_ref[0, g]) / denom
        o_ref[0, :, 2 * pair * HEAD_DIM:(2 * pair + 2) * HEAD_DIM] = (
            jnp.concatenate([o[:tq], o[tq:]], axis=1).astype(BF16))


def _attention(q4, k4, v4, tq):
    B, H, T, _ = q4.shape
    S = k4.shape[2]
    kv = pl.BlockSpec((1, N_KV_HEADS, S, HEAD_DIM), lambda b, i: (b, 0, 0, 0))
    return pl.pallas_call(
        functools.partial(_attn_kernel, tq=tq),
        grid=(B, T // tq),
        in_specs=[pl.BlockSpec((1, H, tq, HEAD_DIM), lambda b, i: (b, 0, i, 0)), kv, kv],
        out_specs=pl.BlockSpec((1, tq, H * HEAD_DIM), lambda b, i: (b, i, 0)),
        out_shape=jax.ShapeDtypeStruct((B, T, H * HEAD_DIM), BF16),
        compiler_params=pltpu.CompilerParams(dimension_semantics=("arbitrary", "arbitrary"),
                                             vmem_limit_bytes=VMEM_LIMIT),
        name="attention",
    )(q4, k4, v4)


def _outproj_kernel(attn_ref, lru_ref, x_ref, wout_ref, gpm_ref, g1_ref, gpf_ref, sh_ref, sc_ref, wq_ref,
                    xmid_ref, hx_ref, qp_ref):
    mix = _dot(attn_ref[0], wout_ref[:D_ATTN, :]) + _dot(lru_ref[0], wout_ref[D_ATTN:, :])
    x_mid = x_ref[0] + g1_ref[0] * (mix * _rms_scale(mix) * gpm_ref[...])
    xmid_ref[0] = x_mid
    hx = (x_mid * _rms_scale(x_mid) * gpf_ref[...] * (1.0 + sc_ref[0]) + sh_ref[0]).astype(BF16)
    hx_ref[0] = hx
    qp_ref[0] = _dot(hx, wq_ref[...]).astype(BF16)


def _outproj(attn, lru, x, wout_bf, gpm, g1, gpf, sh2, sc2, wq_bf, tm):
    B, T, D = x.shape
    nq = wq_bf.shape[1]
    tok = lambda n: pl.BlockSpec((1, tm, n), lambda b, i: (b, i, 0))
    vec = pl.BlockSpec((1, 1, D), lambda b, i: (b, 0, 0))
    return pl.pallas_call(
        _outproj_kernel,
        grid=(B, T // tm),
        in_specs=[tok(D_ATTN), tok(D_LRU), tok(D), _full(wout_bf.shape), _full((1, D)), vec, _full((1, D)),
                  vec, vec, _full(wq_bf.shape)],
        out_specs=[tok(D), tok(D), tok(nq)],
        out_shape=[jax.ShapeDtypeStruct((B, T, D), F32), jax.ShapeDtypeStruct((B, T, D), BF16),
                   jax.ShapeDtypeStruct((B, T, nq), BF16)],
        compiler_params=pltpu.CompilerParams(dimension_semantics=("arbitrary", "arbitrary"),
                                             vmem_limit_bytes=VMEM_LIMIT),
        name="outproj",
    )(attn, lru, x, wout_bf, gpm, g1, gpf, sh2, sc2, wq_bf)


NEG_INF = float("-inf")
BIG = 1e9
SORT_GROUP = 4


def _slab_argmax(vs, cs):
    while len(vs) > 1:
        nv, nc = [], []
        for i in range(0, len(vs) - 1, 2):
            take = vs[i + 1] > vs[i]
            nv.append(jnp.where(take, vs[i + 1], vs[i]))
            nc.append(jnp.where(take, cs[i + 1], cs[i]))
        if len(vs) % 2:
            nv.append(vs[-1])
            nc.append(cs[-1])
        vs, cs = nv, nc
    return vs[0], cs[0]


def _extract_topk(vs, cs, k):
    n = vs[0].shape[1]
    slot = lax.broadcasted_iota(jnp.int32, (k, n), 0)
    top_v = jnp.zeros((k, n), F32)
    top_c = jnp.zeros((k, n), F32)
    groups = []
    for i in range(0, len(vs), SORT_GROUP):
        gv, gc = list(vs[i:i + SORT_GROUP]), list(cs[i:i + SORT_GROUP])
        for end in range(len(gv) - 1, 0, -1):
            for a in range(end):
                take = gv[a + 1] > gv[a]
                gv[a], gv[a + 1] = jnp.where(take, gv[a + 1], gv[a]), jnp.where(take, gv[a], gv[a + 1])
                gc[a], gc[a + 1] = jnp.where(take, gc[a + 1], gc[a]), jnp.where(take, gc[a], gc[a + 1])
        groups.append((gv, gc))
    for j in range(k):
        v, c = _slab_argmax([gv[0] for gv, _ in groups], [gc[0] for _, gc in groups])
        m = jnp.max(v, axis=0, keepdims=True)
        cm = jnp.min(jnp.where(v == m, c, BIG), axis=0, keepdims=True)
        top_v = jnp.where(slot == j, m, top_v)
        top_c = jnp.where(slot == j, cm, top_c)
        for gv, gc in groups:
            hit = gc[0] == cm
            for d in range(len(gv) - 1):
                gv[d] = jnp.where(hit, gv[d + 1], gv[d])
                gc[d] = jnp.where(hit, gc[d + 1], gc[d])
            gv[-1] = jnp.where(hit, NEG_INF, gv[-1])
    return top_v, top_c


def _lookup(table, pos):
    out = jnp.zeros(pos.shape, table.dtype)
    for i in range(PEER_TOPK):
        out = jnp.where(pos == i, table[i:i + 1, :], out)
    return out


def _topk_kernel(qp_ref, keys_ref, ia_ref, ib_ref, g_ref, *, tn):
    half = SUBLANES
    sub = lax.broadcasted_iota(jnp.int32, (half, tn), 0).astype(F32)
    key_codes = [sub + float(half * s) for s in range(PEER_NKEYS // half)]
    cand_codes = [sub, sub + float(half)] + [sub + 16.0 * i for i in range(1, half)] + [(sub + float(half)) * 16.0]
    for h in range(PEER_HEADS):
        tops = []
        for p in range(2):
            col = (2 * h + p) * PEER_NKEYS
            s = _dot_nt(keys_ref[2 * h + p], qp_ref[:, col:col + PEER_NKEYS])
            tops.append(_extract_topk([s[half * r:half * (r + 1), :] for r in range(PEER_NKEYS // half)],
                                      key_codes, PEER_TOPK))
        (v1, i1), (v2, i2) = tops
        groups = [v1[0:1, :] + v2[:half, :], v1[0:1, :] + v2[half:, :]]
        groups += [v1[i:i + 1, :] + v2[:half, :] for i in range(1, half)]
        groups += [v1[half:, :] + v2[0:1, :]]
        top_s, top_c = _extract_topk(groups, cand_codes, PEER_TOPK)
        e = jnp.exp(top_s - top_s[0:1, :])
        rows = slice(h * PEER_TOPK, (h + 1) * PEER_TOPK)
        g_ref[rows, :] = e / jnp.sum(e, axis=0, keepdims=True)
        pair = top_c.astype(jnp.int32)
        ia_ref[rows, :] = _lookup(i1, pair >> 4).astype(jnp.int32)
        ib_ref[rows, :] = _lookup(i2, pair & 15).astype(jnp.int32)


def _peer_topk(qp, keys_bf, tn):
    n, nq = qp.shape
    out = pl.BlockSpec((N_SEL, tn), lambda i: (0, i))
    return pl.pallas_call(
        functools.partial(_topk_kernel, tn=tn),
        grid=(n // tn,),
        in_specs=[pl.BlockSpec((tn, nq), lambda i: (i, 0)), _full(keys_bf.shape)],
        out_specs=[out, out, out],
        out_shape=[jax.ShapeDtypeStruct((N_SEL, n), jnp.int32), jax.ShapeDtypeStruct((N_SEL, n), jnp.int32),
                   jax.ShapeDtypeStruct((N_SEL, n), F32)],
        compiler_params=pltpu.CompilerParams(dimension_semantics=("arbitrary",), vmem_limit_bytes=VMEM_LIMIT),
        name="peer_topk",
    )(qp, keys_bf)


EXPERT_CHUNK = 2048
PEER_VMEM_LIMIT = 62 * 1024 * 1024
GROUPS_PER_CHUNK = EXPERT_CHUNK // PEER_NKEYS
TOKEN_UNROLL = 8


def _peer_kernel(hx_ref, ia_ref, ib_ref, g_ref, ut_ref, v_ref, xmid_ref, g2_ref, gpost_ref, o_ref,
                 s_ref, *, tb, n_chunks):
    j = pl.program_id(1)
    srow = tb + SUBLANES

    def group_rows(gidx):
        return pl.ds(pl.multiple_of(gidx * srow, SUBLANES), tb)

    @pl.when(j < n_chunks)
    def _scores():
        act = jax.nn.gelu(_dot(hx_ref[...], ut_ref[0]))
        for l in range(GROUPS_PER_CHUNK):
            s_ref[group_rows(j * GROUPS_PER_CHUNK + l), :] = act[:, l * PEER_NKEYS:(l + 1) * PEER_NKEYS]

    @pl.when(j == n_chunks)
    def _select():
        sub = lax.broadcasted_iota(jnp.int32, (PEER_NKEYS, N_SEL), 0)

        def tokens(tt, carry):
            for u in range(TOKEN_UNROLL):
                t = tt * TOKEN_UNROLL + u
                first = jnp.where(sub == ia_ref[t], g_ref[t], 0.0).astype(BF16)
                second = jnp.where(sub == ib_ref[t], 1.0, 0.0).astype(BF16)
                rows = pl.ds(t, PEER_NKEYS, stride=srow)
                s_ref[rows, :] = _dot_nt(first, second) * s_ref[rows, :]
            return carry

        lax.fori_loop(0, tb // TOKEN_UNROLL, tokens, 0)

    @pl.when(j >= n_chunks)
    def _combine():
        jj = j - n_chunks
        h = jnp.concatenate([s_ref[group_rows(jj * GROUPS_PER_CHUNK + l), :] for l in range(GROUPS_PER_CHUNK)],
                            axis=1)
        part = _dot(h.astype(BF16), v_ref[...])

        @pl.when(jj == 0)
        def _():
            o_ref[...] = part

        @pl.when(jj > 0)
        def _():
            o_ref[...] += part

    @pl.when(j == 2 * n_chunks - 1)
    def _finish():
        f = o_ref[...]
        o_ref[...] = xmid_ref[...] + g2_ref[0] * (f * _rms_scale(f) * gpost_ref[...])


def _peer_dense(hx, ia3, ib3, g3, ut_bf, v_bf, xmid, g2, gpost, tb, tokens_per_batch):
    n, d = hx.shape
    n_chunks = ut_bf.shape[0]
    bpb = tokens_per_batch // tb
    tok = pl.BlockSpec((tb, d), lambda i, j: (i, 0))
    tok_once = pl.BlockSpec((tb, d), lambda i, j: (i, 0), pipeline_mode=pl.Buffered(1))
    pick = pl.BlockSpec((tb, 1, N_SEL), lambda i, j: (i, 0, 0))
    return pl.pallas_call(
        functools.partial(_peer_kernel, tb=tb, n_chunks=n_chunks),
        grid=(n // tb, 2 * n_chunks),
        in_specs=[tok_once, pick, pick, pick,
                  pl.BlockSpec((1, d, EXPERT_CHUNK), lambda i, j: (jnp.minimum(j, n_chunks - 1), 0, 0)),
                  pl.BlockSpec((EXPERT_CHUNK, d), lambda i, j: (jnp.maximum(j - n_chunks, 0), 0)),
                  tok_once, pl.BlockSpec((1, 1, d), lambda i, j: (i // bpb, 0, 0)), _full((1, d))],
        out_specs=tok,
        out_shape=jax.ShapeDtypeStruct((n, d), F32),
        scratch_shapes=[pltpu.VMEM((PEER_NKEYS * (tb + SUBLANES), PEER_NKEYS), F32)],
        compiler_params=pltpu.CompilerParams(dimension_semantics=("arbitrary", "arbitrary"),
                                             vmem_limit_bytes=PEER_VMEM_LIMIT),
        name="peer_dense",
    )(hx, ia3, ib3, g3, ut_bf, v_bf, xmid, g2, gpost)


def _rope_tables(T):
    t = np.arange(T)
    inv = ROPE_THETA ** (-jnp.arange(ROPE_FREQS, dtype=F32) / ROPE_FREQS)
    ang_r = jnp.asarray(t // GRID_W, F32)[:, None] * inv
    ang_c = jnp.asarray(t % GRID_W, F32)[:, None] * inv
    cos = jnp.concatenate([jnp.cos(ang_r)] * 2 + [jnp.cos(ang_c)] * 2, axis=1)
    sin = jnp.concatenate([-jnp.sin(ang_r), jnp.sin(ang_r), -jnp.sin(ang_c), jnp.sin(ang_c)], axis=1)
    return jnp.concatenate([cos, cos], axis=1), jnp.concatenate([sin, sin], axis=1)


def _block_diag(w):
    n, m, _ = w.shape
    eye = jnp.eye(n, dtype=w.dtype)
    return (eye[:, None, :, None] * w[:, :, None, :]).reshape(n * m, n * m)


def kernel(x, c, ctx, c_ctx, w_mod, b_mod, g_pre_mix, g_post_mix, g_pre_ffn, g_post_ffn, w_in, q_norm_g, k_norm_g, conv_w, conv_b, rg_wa, rg_ba, rg_wi, rg_bi, rg_lambda, w_out, peer_wq, peer_subkeys, peer_u, peer_v):
    B, T, D = x.shape
    C = ctx.shape[1]
    depth = w_mod.shape[0]
    cos, sin = _rope_tables(T)
    head_ones = _block_diag(jnp.ones((N_Q_HEADS, HEAD_DIM, HEAD_DIM), BF16))
    mod_rows = -(-(B + 1) // SUBLANES) * SUBLANES

    for l in range(depth):
        assert l == depth - 1, "context-stream update between layers is not implemented"
        c_all = jnp.zeros((mod_rows, D), F32).at[:B].set(c).at[B].set(c_ctx)
        mod = _modulation(c_all, w_mod[l], b_mod[l])
        mx = mod[:B].reshape(B, N_MOD, 1, D)
        sh1, sc1, g1, sh2, sc2, g2 = (mx[:, i] for i in range(N_MOD))
        mc = mod[B].reshape(N_MOD, 1, D)

        w_bf = w_in[l].astype(BF16)
        qg = jnp.tile(q_norm_g[l], N_Q_HEADS).reshape(1, D_ATTN)
        kg = jnp.tile(k_norm_g[l], N_KV_HEADS).reshape(1, D_KV)
        gpre = g_pre_mix[l].reshape(1, D)
        q, kx, vx, lx, gz = _inproj_latent(x, sh1, sc1, gpre, w_bf, cos, sin, qg, kg, head_ones, tm=512)
        kc, vc, lc = _inproj_context(ctx, mc[0], mc[1], gpre, w_bf[:, D_ATTN:D_ATTN + 2 * D_KV + D_LRU],
                                     kg, head_ones[:D_KV, :D_KV], tm=C)

        lru = _rglru(lc, lx, gz, conv_w[l], conv_b[l].reshape(1, D_LRU),
                     jnp.stack([_block_diag(rg_wa[l, d]) for d in range(2)]).astype(BF16),
                     jnp.stack([_block_diag(rg_wi[l, d]) for d in range(2)]).astype(BF16),
                     rg_ba[l].reshape(2, 1, D_LRU), rg_bi[l].reshape(2, 1, D_LRU), rg_lambda[l].reshape(2, 1, D_LRU))

        attn = _attention(q, jnp.concatenate([kc, kx], axis=2), jnp.concatenate([vc, vx], axis=2), tq=256)

        x_mid, hx, qp = _outproj(attn, lru, x, w_out[l].astype(BF16), g_post_mix[l].reshape(1, D), g1,
                                 g_pre_ffn[l].reshape(1, D), sh2, sc2, peer_wq[l].astype(BF16), tm=256)

        n = B * T
        keys_bf = peer_subkeys[l].reshape(2 * PEER_HEADS, PEER_NKEYS, -1).astype(BF16)
        ia, ib, gate = _peer_topk(qp.reshape(n, -1), keys_bf, tn=256)
        picks = lambda a: a.T.reshape(n, 1, N_SEL)
        ut = peer_u[l].astype(BF16).reshape(-1, EXPERT_CHUNK, D).transpose(0, 2, 1)
        x = _peer_dense(hx.reshape(n, D), picks(ia), picks(ib), picks(gate), ut,
                        peer_v[l].astype(BF16), x_mid.reshape(n, D), g2, g_post_ffn[l].reshape(1, D),
                        tb=min(512, T), tokens_per_batch=T).reshape(B, T, D)
    return x
```

```python
import functools

import jax
import jax.numpy as jnp
import numpy as np
from jax import lax
from jax.experimental import pallas as pl
from jax.experimental.pallas import tpu as pltpu

F32 = jnp.float32
BF16 = jnp.bfloat16

HEAD_DIM = 64
N_Q_HEADS = 8
N_KV_HEADS = 2
D_ATTN = 512
D_KV = 128
D_LRU = 512
LRU_BLOCKS = 8
CONV_W = 4
CONV_PAD_LEFT = 2
RG_C = 8.0
ROPE_THETA = 10000.0
ROPE_FREQS = 16
GRID_W = 64
N_MOD = 6
EPS = 1e-6
PEER_HEADS = 8
PEER_NKEYS = 128
PEER_TOPK = 16
N_SEL = PEER_HEADS * PEER_TOPK

SUBLANES = 8
LANES = 128
VMEM_LIMIT = 56 * 1024 * 1024

LOG2_E = 1.4426950408889634
NT_DIMS = (((1,), (1,)), ((), ()))


def _dot(a, b):
    return jnp.dot(a, b, preferred_element_type=F32)


def _dot_nt(a, b):
    return lax.dot_general(a, b, NT_DIMS, preferred_element_type=F32)


def _split_bf16(x):
    hi = x.astype(BF16)
    return hi, (x - hi.astype(F32)).astype(BF16)


def _gelu_tanh(x):
    k = -2.0 * LOG2_E * (2.0 / np.pi) ** 0.5
    return x / (1.0 + jnp.exp2(x * (k + (k * 0.044715) * (x * x))))


def _rms_scale(x):
    return lax.rsqrt(jnp.mean(x * x, axis=-1, keepdims=True) + EPS)


def _mod_kernel(c_ref, w_ref, b_ref, o_ref):
    s = jax.nn.silu(c_ref[...])
    s_hi, s_lo = _split_bf16(s)
    w_hi, w_lo = _split_bf16(w_ref[...])
    o_ref[...] = _dot(s_hi, w_hi) + _dot(s_lo, w_hi) + _dot(s_hi, w_lo) + b_ref[...]


def _modulation(c_all, w_mod, b_mod):
    m, d = c_all.shape
    n = w_mod.shape[1]
    tn = 1536
    return pl.pallas_call(
        _mod_kernel,
        grid=(n // tn,),
        in_specs=[pl.BlockSpec((m, d), lambda j: (0, 0)),
                  pl.BlockSpec((d, tn), lambda j: (0, j)),
                  pl.BlockSpec((1, tn), lambda j: (0, j))],
        out_specs=pl.BlockSpec((m, tn), lambda j: (0, j)),
        out_shape=jax.ShapeDtypeStruct((m, n), F32),
        compiler_params=pltpu.CompilerParams(dimension_semantics=("arbitrary",), vmem_limit_bytes=VMEM_LIMIT),
        name="modulation",
    )(c_all, w_mod, b_mod.reshape(1, n))


def _head_rms(y, gain, bd):
    hi, lo = _split_bf16(y * y)
    ss = _dot(hi, bd) + _dot(lo, bd)
    return y * lax.rsqrt(ss * (1.0 / HEAD_DIM) + EPS) * gain


def _rope(y, cos, sin):
    n = y.shape[-1]
    lane = lax.broadcasted_iota(jnp.int32, y.shape, 1)
    first_half = (lane % (2 * ROPE_FREQS)) < ROPE_FREQS
    swapped = jnp.where(first_half, pltpu.roll(y, n - ROPE_FREQS, 1), pltpu.roll(y, ROPE_FREQS, 1))
    return y * cos + swapped * sin


def _inproj_latent_kernel(x_ref, sh_ref, sc_ref, g_ref, w_ref, cos_ref, sin_ref, qg_ref, kg_ref, bd_ref,
                          q_ref, k_ref, v_ref, l_ref, gz_ref):
    x = x_ref[0]
    h = x * _rms_scale(x) * g_ref[...] * (1.0 + sc_ref[0]) + sh_ref[0]
    p = _dot(h.astype(BF16), w_ref[...])
    cos, sin = cos_ref[...], sin_ref[...]
    bd = bd_ref[...]
    q = _head_rms(p[:, :D_ATTN], qg_ref[...], bd)
    q = _rope(q, jnp.concatenate([cos] * 4, axis=1), jnp.concatenate([sin] * 4, axis=1)) * (HEAD_DIM ** -0.5 * LOG2_E)
    _store_heads(q_ref, q, N_Q_HEADS)
    k = _head_rms(p[:, D_ATTN:D_ATTN + D_KV], kg_ref[...], bd[:D_KV, :D_KV])
    _store_heads(k_ref, _rope(k, cos, sin), N_KV_HEADS)
    _store_heads(v_ref, p[:, D_ATTN + D_KV:D_ATTN + 2 * D_KV], N_KV_HEADS)
    l_ref[0] = p[:, D_ATTN + 2 * D_KV:D_ATTN + 2 * D_KV + D_LRU]
    gz_ref[0] = jax.nn.gelu(p[:, D_ATTN + 2 * D_KV + D_LRU:])


def _store_heads(ref, y, n_heads):
    for h in range(n_heads):
        ref[0, h] = y[:, h * HEAD_DIM:(h + 1) * HEAD_DIM].astype(BF16)


def _inproj_context_kernel(x_ref, sh_ref, sc_ref, g_ref, w_ref, kg_ref, bd_ref, k_ref, v_ref, l_ref):
    x = x_ref[0]
    h = x * _rms_scale(x) * g_ref[...] * (1.0 + sc_ref[...]) + sh_ref[...]
    p = _dot(h.astype(BF16), w_ref[...])
    _store_heads(k_ref, _head_rms(p[:, :D_KV], kg_ref[...], bd_ref[...]), N_KV_HEADS)
    _store_heads(v_ref, p[:, D_KV:2 * D_KV], N_KV_HEADS)
    l_ref[0] = p[:, 2 * D_KV:]


def _full(shape):
    nd = len(shape)
    return pl.BlockSpec(shape, lambda *_: (0,) * nd)


def _inproj_latent(x, sh, sc, gpre, w_bf, cos, sin, qg, kg, bd, tm):
    B, T, D = x.shape
    tok = lambda n: pl.BlockSpec((1, tm, n), lambda b, i: (b, i, 0))
    vec = pl.BlockSpec((1, 1, D), lambda b, i: (b, 0, 0))
    rope = pl.BlockSpec((tm, LANES), lambda b, i: (i, 0))
    heads = lambda n: pl.BlockSpec((1, n, tm, HEAD_DIM), lambda b, i: (b, 0, i, 0))
    head_shape = lambda n: jax.ShapeDtypeStruct((B, n, T, HEAD_DIM), BF16)
    return pl.pallas_call(
        _inproj_latent_kernel,
        grid=(B, T // tm),
        in_specs=[tok(D), vec, vec, _full((1, D)), _full(w_bf.shape), rope, rope,
                  _full((1, D_ATTN)), _full((1, D_KV)), _full((D_ATTN, D_ATTN))],
        out_specs=[heads(N_Q_HEADS), heads(N_KV_HEADS), heads(N_KV_HEADS), tok(D_LRU), tok(D_LRU)],
        out_shape=[head_shape(N_Q_HEADS), head_shape(N_KV_HEADS), head_shape(N_KV_HEADS),
                   jax.ShapeDtypeStruct((B, T, D_LRU), F32), jax.ShapeDtypeStruct((B, T, D_LRU), F32)],
        compiler_params=pltpu.CompilerParams(dimension_semantics=("arbitrary", "arbitrary"),
                                             vmem_limit_bytes=VMEM_LIMIT),
        name="inproj_latent",
    )(x, sh, sc, gpre, w_bf, cos, sin, qg, kg, bd)


def _inproj_context(ctx, sh, sc, gpre, w_bf, kg, bd, tm):
    B, C, D = ctx.shape
    tok = lambda n: pl.BlockSpec((1, tm, n), lambda b, i: (b, i, 0))
    heads = pl.BlockSpec((1, N_KV_HEADS, tm, HEAD_DIM), lambda b, i: (b, 0, i, 0))
    head_shape = jax.ShapeDtypeStruct((B, N_KV_HEADS, C, HEAD_DIM), BF16)
    return pl.pallas_call(
        _inproj_context_kernel,
        grid=(B, C // tm),
        in_specs=[tok(D), _full((1, D)), _full((1, D)), _full((1, D)), _full(w_bf.shape),
                  _full((1, D_KV)), _full((D_KV, D_KV))],
        out_specs=[heads, heads, tok(D_LRU)],
        out_shape=[head_shape, head_shape, jax.ShapeDtypeStruct((B, C, D_LRU), F32)],
        compiler_params=pltpu.CompilerParams(dimension_semantics=("arbitrary", "arbitrary"),
                                             vmem_limit_bytes=VMEM_LIMIT),
        name="inproj_context",
    )(ctx, sh, sc, gpre, w_bf, kg, bd)


GATE_ROWS = 256
HALO = SUBLANES


def _conv_chunk(src_ref, start, n_rows, cw, cb):
    lo, hi = start - HALO, start + GATE_ROWS + HALO
    parts = []
    if lo < 0:
        parts.append(jnp.zeros((HALO, D_LRU), F32))
        lo = 0
    body_hi = min(hi, n_rows)
    parts.append(src_ref[0, lo:body_hi, :])
    if hi > n_rows:
        parts.append(jnp.zeros((HALO, D_LRU), F32))
    xh = jnp.concatenate(parts, axis=0) if len(parts) > 1 else parts[0]
    out = cb
    for j in range(CONV_W):
        off = HALO + j - CONV_PAD_LEFT
        out = out + xh[off:off + GATE_ROWS, :] * cw[j:j + 1, :]
    return out


def _tile_scan(a, b, carry, row, reverse):
    for sh in (1, 2, 4):
        if reverse:
            a_n, b_n = pltpu.roll(a, SUBLANES - sh, 0), pltpu.roll(b, SUBLANES - sh, 0)
            valid = row < SUBLANES - sh
        else:
            a_n, b_n = pltpu.roll(a, sh, 0), pltpu.roll(b, sh, 0)
            valid = row >= sh
        b = jnp.where(valid, a * b_n + b, b)
        a = jnp.where(valid, a * a_n, a)
    h = a * carry + b
    new_carry = h[0:1, :] if reverse else h[SUBLANES - 1:SUBLANES, :]
    return h, new_carry


def _rglru_kernel(lc_ref, lx_ref, gz_ref, cw_ref, cb_ref, wa_ref, wi_ref, ba_ref, bi_ref, lam_ref, o_ref,
                  a0_ref, b0_ref, a1_ref, b1_ref, *, n_ctx, n_lat):
    cw, cb = cw_ref[...], cb_ref[...]
    a_refs, b_refs = (a0_ref, a1_ref), (b0_ref, b1_ref)
    for seg_ref, seg_rows, seg_off in ((lc_ref, n_ctx, 0), (lx_ref, n_lat, n_ctx)):
        for s in range(0, seg_rows, GATE_ROWS):
            xc = _conv_chunk(seg_ref, s, seg_rows, cw, cb)
            xb = xc.astype(BF16)
            for d in range(2):
                r = jax.nn.sigmoid(_dot(xb, wa_ref[d]) + ba_ref[d])
                i = jax.nn.sigmoid(_dot(xb, wi_ref[d]) + bi_ref[d])
                log_a = -RG_C * r * jax.nn.softplus(-lam_ref[d])
                a = jnp.exp(log_a)
                b = jnp.sqrt(jnp.maximum(1.0 - a * a, 0.0)) * (i * xc)
                a_refs[d][seg_off + s:seg_off + s + GATE_ROWS, :] = a
                b_refs[d][seg_off + s:seg_off + s + GATE_ROWS, :] = b

    ctx_tiles, all_tiles = n_ctx // SUBLANES, (n_ctx + n_lat) // SUBLANES
    row = lax.broadcasted_iota(jnp.int32, (SUBLANES, D_LRU), 0)

    def step(j, carry):
        cf, cr = carry
        tf = pl.multiple_of(j * SUBLANES, SUBLANES)
        jr = jnp.where(j < ctx_tiles, ctx_tiles - 1 - j, all_tiles + ctx_tiles - 1 - j)
        tr = pl.multiple_of(jr * SUBLANES, SUBLANES)
        hf, cf = _tile_scan(a0_ref[pl.ds(tf, SUBLANES), :], b0_ref[pl.ds(tf, SUBLANES), :], cf, row, False)
        hr, cr = _tile_scan(a1_ref[pl.ds(tr, SUBLANES), :], b1_ref[pl.ds(tr, SUBLANES), :], cr, row, True)
        b0_ref[pl.ds(tf, SUBLANES), :] = hf
        b1_ref[pl.ds(tr, SUBLANES), :] = hr
        return cf, cr

    zero = jnp.zeros((1, D_LRU), F32)
    lax.fori_loop(0, all_tiles, step, (zero, zero))
    for s in range(0, n_lat, GATE_ROWS):
        y = b0_ref[n_ctx + s:n_ctx + s + GATE_ROWS, :] + b1_ref[n_ctx + s:n_ctx + s + GATE_ROWS, :]
        o_ref[0, s:s + GATE_ROWS, :] = (y * gz_ref[0, s:s + GATE_ROWS, :]).astype(BF16)


def _rglru(lc, lx, gz, conv_w, conv_b, wa_bd, wi_bd, ba, bi, lam):
    B, C, _ = lc.shape
    T = lx.shape[1]
    seq = lambda n: pl.BlockSpec((1, n, D_LRU), lambda b: (b, 0, 0))
    scratch = [pltpu.VMEM((C + T, D_LRU), F32) for _ in range(4)]
    return pl.pallas_call(
        functools.partial(_rglru_kernel, n_ctx=C, n_lat=T),
        grid=(B,),
        in_specs=[seq(C), seq(T), seq(T), _full((CONV_W, D_LRU)), _full((1, D_LRU)),
                  _full((2, D_LRU, D_LRU)), _full((2, D_LRU, D_LRU)),
                  _full((2, 1, D_LRU)), _full((2, 1, D_LRU)), _full((2, 1, D_LRU))],
        out_specs=seq(T),
        out_shape=jax.ShapeDtypeStruct((B, T, D_LRU), BF16),
        scratch_shapes=scratch,
        compiler_params=pltpu.CompilerParams(dimension_semantics=("arbitrary",), vmem_limit_bytes=VMEM_LIMIT),
        name="rglru",
    )(lc, lx, gz, conv_w, conv_b, wa_bd, wi_bd, ba, bi, lam)


def _attn_kernel(q_ref, k_ref, v_ref, o_ref, *, tq):
    group = N_Q_HEADS // N_KV_HEADS
    for pair in range(N_Q_HEADS // 2):
        g = 2 * pair // group
        q = q_ref[0, 2 * pair:2 * pair + 2].reshape(2 * tq, HEAD_DIM)
        s = _dot_nt(q, k_ref[0, g])
        p = jnp.exp2(s - jnp.max(s, axis=-1, keepdims=True))
        denom = jnp.sum(p, axis=-1, keepdims=True)
        o = _dot(p.astype(BF16), v_ref[0, g]) / denom
        o_ref[0, :, 2 * pair * HEAD_DIM:(2 * pair + 2) * HEAD_DIM] = (
            jnp.concatenate([o[:tq], o[tq:]], axis=1).astype(BF16))


def _attention(q4, k4, v4, tq):
    B, H, T, _ = q4.shape
    S = k4.shape[2]
    kv = pl.BlockSpec((1, N_KV_HEADS, S, HEAD_DIM), lambda b, i: (b, 0, 0, 0))
    return pl.pallas_call(
        functools.partial(_attn_kernel, tq=tq),
        grid=(B, T // tq),
        in_specs=[pl.BlockSpec((1, H, tq, HEAD_DIM), lambda b, i: (b, 0, i, 0)), kv, kv],
        out_specs=pl.BlockSpec((1, tq, H * HEAD_DIM), lambda b, i: (b, i, 0)),
        out_shape=jax.ShapeDtypeStruct((B, T, H * HEAD_DIM), BF16),
        compiler_params=pltpu.CompilerParams(dimension_semantics=("arbitrary", "arbitrary"),
                                             vmem_limit_bytes=VMEM_LIMIT),
        name="attention",
    )(q4, k4, v4)


def _outproj_kernel(attn_ref, lru_ref, x_ref, wout_ref, gpm_ref, g1_ref, gpf_ref, sh_ref, sc_ref, wq_ref,
                    xmid_ref, hx_ref, qp_ref):
    mix = _dot(attn_ref[0], wout_ref[:D_ATTN, :]) + _dot(lru_ref[0], wout_ref[D_ATTN:, :])
    x_mid = x_ref[0] + g1_ref[0] * (mix * _rms_scale(mix) * gpm_ref[...])
    xmid_ref[0] = x_mid
    hx = (x_mid * _rms_scale(x_mid) * gpf_ref[...] * (1.0 + sc_ref[0]) + sh_ref[0]).astype(BF16)
    hx_ref[0] = hx
    qp_ref[0] = _dot(hx, wq_ref[...]).astype(BF16)


def _outproj(attn, lru, x, wout_bf, gpm, g1, gpf, sh2, sc2, wq_bf, tm):
    B, T, D = x.shape
    nq = wq_bf.shape[1]
    tok = lambda n: pl.BlockSpec((1, tm, n), lambda b, i: (b, i, 0))
    vec = pl.BlockSpec((1, 1, D), lambda b, i: (b, 0, 0))
    return pl.pallas_call(
        _outproj_kernel,
        grid=(B, T // tm),
        in_specs=[tok(D_ATTN), tok(D_LRU), tok(D), _full(wout_bf.shape), _full((1, D)), vec, _full((1, D)),
                  vec, vec, _full(wq_bf.shape)],
        out_specs=[tok(D), tok(D), tok(nq)],
        out_shape=[jax.ShapeDtypeStruct((B, T, D), F32), jax.ShapeDtypeStruct((B, T, D), BF16),
                   jax.ShapeDtypeStruct((B, T, nq), BF16)],
        compiler_params=pltpu.CompilerParams(dimension_semantics=("arbitrary", "arbitrary"),
                                             vmem_limit_bytes=VMEM_LIMIT),
        name="outproj",
    )(attn, lru, x, wout_bf, gpm, g1, gpf, sh2, sc2, wq_bf)


NEG_INF = float("-inf")
BIG = 1e9
SORT_GROUP = 4


def _slab_argmax(vs, cs):
    while len(vs) > 1:
        nv, nc = [], []
        for i in range(0, len(vs) - 1, 2):
            take = vs[i + 1] > vs[i]
            nv.append(jnp.where(take, vs[i + 1], vs[i]))
            nc.append(jnp.where(take, cs[i + 1], cs[i]))
        if len(vs) % 2:
            nv.append(vs[-1])
            nc.append(cs[-1])
        vs, cs = nv, nc
    return vs[0], cs[0]


def _extract_topk(vs, cs, k):
    n = vs[0].shape[1]
    slot = lax.broadcasted_iota(jnp.int32, (k, n), 0)
    top_v = jnp.zeros((k, n), F32)
    top_c = jnp.zeros((k, n), F32)
    groups = []
    for i in range(0, len(vs), SORT_GROUP):
        gv, gc = list(vs[i:i + SORT_GROUP]), list(cs[i:i + SORT_GROUP])
        for end in range(len(gv) - 1, 0, -1):
            for a in range(end):
                take = gv[a + 1] > gv[a]
                gv[a], gv[a + 1] = jnp.where(take, gv[a + 1], gv[a]), jnp.where(take, gv[a], gv[a + 1])
                gc[a], gc[a + 1] = jnp.where(take, gc[a + 1], gc[a]), jnp.where(take, gc[a], gc[a + 1])
        groups.append((gv, gc))
    for j in range(k):
        v, c = _slab_argmax([gv[0] for gv, _ in groups], [gc[0] for _, gc in groups])
        m = jnp.max(v, axis=0, keepdims=True)
        cm = jnp.min(jnp.where(v == m, c, BIG), axis=0, keepdims=True)
        top_v = jnp.where(slot == j, m, top_v)
        top_c = jnp.where(slot == j, cm, top_c)
        for gv, gc in groups:
            hit = gc[0] == cm
            for d in range(len(gv) - 1):
                gv[d] = jnp.where(hit, gv[d + 1], gv[d])
                gc[d] = jnp.where(hit, gc[d + 1], gc[d])
            gv[-1] = jnp.where(hit, NEG_INF, gv[-1])
    return top_v, top_c


def _lookup(table, pos):
    out = jnp.zeros(pos.shape, table.dtype)
    for i in range(PEER_TOPK):
        out = jnp.where(pos == i, table[i:i + 1, :], out)
    return out


def _topk_kernel(qp_ref, keys_ref, ia_ref, ib_ref, g_ref, *, tn):
    half = SUBLANES
    sub = lax.broadcasted_iota(jnp.int32, (half, tn), 0).astype(F32)
    key_codes = [sub + float(half * s) for s in range(PEER_NKEYS // half)]
    cand_codes = [sub, sub + float(half)] + [sub + 16.0 * i for i in range(1, half)] + [(sub + float(half)) * 16.0]
    for h in range(PEER_HEADS):
        tops = []
        for p in range(2):
            col = (2 * h + p) * PEER_NKEYS
            s = _dot_nt(keys_ref[2 * h + p], qp_ref[:, col:col + PEER_NKEYS])
            tops.append(_extract_topk([s[half * r:half * (r + 1), :] for r in range(PEER_NKEYS // half)],
                                      key_codes, PEER_TOPK))
        (v1, i1), (v2, i2) = tops
        groups = [v1[0:1, :] + v2[:half, :], v1[0:1, :] + v2[half:, :]]
        groups += [v1[i:i + 1, :] + v2[:half, :] for i in range(1, half)]
        groups += [v1[half:, :] + v2[0:1, :]]
        top_s, top_c = _extract_topk(groups, cand_codes, PEER_TOPK)
        e = jnp.exp(top_s - top_s[0:1, :])
        rows = slice(h * PEER_TOPK, (h + 1) * PEER_TOPK)
        g_ref[rows, :] = e / jnp.sum(e, axis=0, keepdims=True)
        pair = top_c.astype(jnp.int32)
        ia_ref[rows, :] = _lookup(i1, pair >> 4).astype(jnp.int32)
        ib_ref[rows, :] = _lookup(i2, pair & 15).astype(jnp.int32)


def _peer_topk(qp, keys_bf, tn):
    n, nq = qp.shape
    out = pl.BlockSpec((N_SEL, tn), lambda i: (0, i))
    return pl.pallas_call(
        functools.partial(_topk_kernel, tn=tn),
        grid=(n // tn,),
        in_specs=[pl.BlockSpec((tn, nq), lambda i: (i, 0)), _full(keys_bf.shape)],
        out_specs=[out, out, out],
        out_shape=[jax.ShapeDtypeStruct((N_SEL, n), jnp.int32), jax.ShapeDtypeStruct((N_SEL, n), jnp.int32),
                   jax.ShapeDtypeStruct((N_SEL, n), F32)],
        compiler_params=pltpu.CompilerParams(dimension_semantics=("arbitrary",), vmem_limit_bytes=VMEM_LIMIT),
        name="peer_topk",
    )(qp, keys_bf)


EXPERT_CHUNK = 2048
PEER_VMEM_LIMIT = 62 * 1024 * 1024
GROUPS_PER_CHUNK = EXPERT_CHUNK // PEER_NKEYS
TOKEN_UNROLL = 32


def _peer_kernel(hx_ref, ia_ref, ib_ref, g_ref, ut_ref, v_ref, xmid_ref, g2_ref, gpost_ref, o_ref,
                 s_ref, *, tb, n_chunks):
    j = pl.program_id(1)
    srow = tb + SUBLANES

    def group_rows(gidx):
        return pl.ds(pl.multiple_of(gidx * srow, SUBLANES), tb)

    @pl.when(j < n_chunks)
    def _scores():
        act = _gelu_tanh(_dot(hx_ref[...], ut_ref[0]))
        for l in range(GROUPS_PER_CHUNK):
            s_ref[group_rows(j * GROUPS_PER_CHUNK + l), :] = act[:, l * PEER_NKEYS:(l + 1) * PEER_NKEYS]

    @pl.when(j == n_chunks)
    def _select():
        sub = lax.broadcasted_iota(jnp.int32, (PEER_NKEYS, N_SEL), 0)

        def tokens(tt, carry):
            for u in range(TOKEN_UNROLL):
                t = tt * TOKEN_UNROLL + u
                first = jnp.where(sub == ia_ref[t], g_ref[t], 0.0).astype(BF16)
                second = jnp.where(sub == ib_ref[t], 1.0, 0.0).astype(BF16)
                rows = pl.ds(t, PEER_NKEYS, stride=srow)
                s_ref[rows, :] = _dot_nt(first, second) * s_ref[rows, :]
            return carry

        lax.fori_loop(0, tb // TOKEN_UNROLL, tokens, 0)

    @pl.when(j >= n_chunks)
    def _combine():
        jj = j - n_chunks
        h = jnp.concatenate([s_ref[group_rows(jj * GROUPS_PER_CHUNK + l), :] for l in range(GROUPS_PER_CHUNK)],
                            axis=1)
        part = _dot(h.astype(BF16), v_ref[...])

        @pl.when(jj == 0)
        def _():
            o_ref[...] = part

        @pl.when(jj > 0)
        def _():
            o_ref[...] += part

    @pl.when(j == 2 * n_chunks - 1)
    def _finish():
        f = o_ref[...]
        o_ref[...] = xmid_ref[...] + g2_ref[0] * (f * _rms_scale(f) * gpost_ref[...])


def _peer_dense(hx, ia3, ib3, g3, ut_bf, v_bf, xmid, g2, gpost, tb, tokens_per_batch):
    n, d = hx.shape
    n_chunks = ut_bf.shape[0]
    bpb = tokens_per_batch // tb
    tok = pl.BlockSpec((tb, d), lambda i, j: (i, 0))
    tok_once = pl.BlockSpec((tb, d), lambda i, j: (i, 0), pipeline_mode=pl.Buffered(1))
    pick = pl.BlockSpec((tb, 1, N_SEL), lambda i, j: (i, 0, 0))
    return pl.pallas_call(
        functools.partial(_peer_kernel, tb=tb, n_chunks=n_chunks),
        grid=(n // tb, 2 * n_chunks),
        in_specs=[tok_once, pick, pick, pick,
                  pl.BlockSpec((1, d, EXPERT_CHUNK), lambda i, j: (jnp.minimum(j, n_chunks - 1), 0, 0)),
                  pl.BlockSpec((EXPERT_CHUNK, d), lambda i, j: (jnp.maximum(j - n_chunks, 0), 0)),
                  tok_once, pl.BlockSpec((1, 1, d), lambda i, j: (i // bpb, 0, 0)), _full((1, d))],
        out_specs=tok,
        out_shape=jax.ShapeDtypeStruct((n, d), F32),
        scratch_shapes=[pltpu.VMEM((PEER_NKEYS * (tb + SUBLANES), PEER_NKEYS), F32)],
        compiler_params=pltpu.CompilerParams(dimension_semantics=("arbitrary", "arbitrary"),
                                             vmem_limit_bytes=PEER_VMEM_LIMIT),
        name="peer_dense",
    )(hx, ia3, ib3, g3, ut_bf, v_bf, xmid, g2, gpost)


def _rope_tables(T):
    t = np.arange(T)
    inv = ROPE_THETA ** (-jnp.arange(ROPE_FREQS, dtype=F32) / ROPE_FREQS)
    ang_r = jnp.asarray(t // GRID_W, F32)[:, None] * inv
    ang_c = jnp.asarray(t % GRID_W, F32)[:, None] * inv
    cos = jnp.concatenate([jnp.cos(ang_r)] * 2 + [jnp.cos(ang_c)] * 2, axis=1)
    sin = jnp.concatenate([-jnp.sin(ang_r), jnp.sin(ang_r), -jnp.sin(ang_c), jnp.sin(ang_c)], axis=1)
    return jnp.concatenate([cos, cos], axis=1), jnp.concatenate([sin, sin], axis=1)


def _block_diag(w):
    n, m, _ = w.shape
    eye = jnp.eye(n, dtype=w.dtype)
    return (eye[:, None, :, None] * w[:, :, None, :]).reshape(n * m, n * m)


def kernel(x, c, ctx, c_ctx, w_mod, b_mod, g_pre_mix, g_post_mix, g_pre_ffn, g_post_ffn, w_in, q_norm_g, k_norm_g, conv_w, conv_b, rg_wa, rg_ba, rg_wi, rg_bi, rg_lambda, w_out, peer_wq, peer_subkeys, peer_u, peer_v):
    B, T, D = x.shape
    C = ctx.shape[1]
    depth = w_mod.shape[0]
    cos, sin = _rope_tables(T)
    head_ones = _block_diag(jnp.ones((N_Q_HEADS, HEAD_DIM, HEAD_DIM), BF16))
    mod_rows = -(-(B + 1) // SUBLANES) * SUBLANES

    for l in range(depth):
        assert l == depth - 1, "context-stream update between layers is not implemented"
        c_all = jnp.zeros((mod_rows, D), F32).at[:B].set(c).at[B].set(c_ctx)
        mod = _modulation(c_all, w_mod[l], b_mod[l])
        mx = mod[:B].reshape(B, N_MOD, 1, D)
        sh1, sc1, g1, sh2, sc2, g2 = (mx[:, i] for i in range(N_MOD))
        mc = mod[B].reshape(N_MOD, 1, D)

        w_bf = w_in[l].astype(BF16)
        qg = jnp.tile(q_norm_g[l], N_Q_HEADS).reshape(1, D_ATTN)
        kg = jnp.tile(k_norm_g[l], N_KV_HEADS).reshape(1, D_KV)
        gpre = g_pre_mix[l].reshape(1, D)
        q, kx, vx, lx, gz = _inproj_latent(x, sh1, sc1, gpre, w_bf, cos, sin, qg, kg, head_ones, tm=512)
        kc, vc, lc = _inproj_context(ctx, mc[0], mc[1], gpre, w_bf[:, D_ATTN:D_ATTN + 2 * D_KV + D_LRU],
                                     kg, head_ones[:D_KV, :D_KV], tm=C)

        lru = _rglru(lc, lx, gz, conv_w[l], conv_b[l].reshape(1, D_LRU),
                     jnp.stack([_block_diag(rg_wa[l, d]) for d in range(2)]).astype(BF16),
                     jnp.stack([_block_diag(rg_wi[l, d]) for d in range(2)]).astype(BF16),
                     rg_ba[l].reshape(2, 1, D_LRU), rg_bi[l].reshape(2, 1, D_LRU), rg_lambda[l].reshape(2, 1, D_LRU))

        attn = _attention(q, jnp.concatenate([kc, kx], axis=2), jnp.concatenate([vc, vx], axis=2), tq=256)

        x_mid, hx, qp = _outproj(attn, lru, x, w_out[l].astype(BF16), g_post_mix[l].reshape(1, D), g1,
                                 g_pre_ffn[l].reshape(1, D), sh2, sc2, peer_wq[l].astype(BF16), tm=256)

        n = B * T
        keys_bf = peer_subkeys[l].reshape(2 * PEER_HEADS, PEER_NKEYS, -1).astype(BF16)
        ia, ib, gate = _peer_topk(qp.reshape(n, -1), keys_bf, tn=256)
        picks = lambda a: a.T.reshape(n, 1, N_SEL)
        ut = peer_u[l].astype(BF16).reshape(-1, EXPERT_CHUNK, D).transpose(0, 2, 1)
        x = _peer_dense(hx.reshape(n, D), picks(ia), picks(ib), picks(gate), ut,
                        peer_v[l].astype(BF16), x_mid.reshape(n, D), g2, g_post_ffn[l].reshape(1, D),
                        tb=min(512, T), tokens_per_batch=T).reshape(B, T, D)
    return x
```

```python
import functools

import jax
import jax.numpy as jnp
import numpy as np
from jax import lax
from jax.experimental import pallas as pl
from jax.experimental.pallas import tpu as pltpu

F32 = jnp.float32
BF16 = jnp.bfloat16

HEAD_DIM = 64
N_Q_HEADS = 8
N_KV_HEADS = 2
D_ATTN = 512
D_KV = 128
D_LRU = 512
LRU_BLOCKS = 8
CONV_W = 4
CONV_PAD_LEFT = 2
RG_C = 8.0
ROPE_THETA = 10000.0
ROPE_FREQS = 16
GRID_W = 64
N_MOD = 6
EPS = 1e-6
PEER_HEADS = 8
PEER_NKEYS = 128
PEER_TOPK = 16
N_SEL = PEER_HEADS * PEER_TOPK

SUBLANES = 8
LANES = 128
VMEM_LIMIT = 56 * 1024 * 1024

LOG2_E = 1.4426950408889634
NT_DIMS = (((1,), (1,)), ((), ()))


def _dot(a, b):
    return jnp.dot(a, b, preferred_element_type=F32)


def _dot_nt(a, b):
    return lax.dot_general(a, b, NT_DIMS, preferred_element_type=F32)


def _split_bf16(x):
    hi = x.astype(BF16)
    return hi, (x - hi.astype(F32)).astype(BF16)


def _gelu_tanh(x):
    k = -2.0 * LOG2_E * (2.0 / np.pi) ** 0.5
    return x / (1.0 + jnp.exp2(x * (k + (k * 0.044715) * (x * x))))


def _rms_scale(x):
    return lax.rsqrt(jnp.mean(x * x, axis=-1, keepdims=True) + EPS)


def _mod_kernel(c_ref, w_ref, b_ref, o_ref):
    s = jax.nn.silu(c_ref[...])
    s_hi, s_lo = _split_bf16(s)
    w_hi, w_lo = _split_bf16(w_ref[...])
    o_ref[...] = _dot(s_hi, w_hi) + _dot(s_lo, w_hi) + _dot(s_hi, w_lo) + b_ref[...]


def _modulation(c_all, w_mod, b_mod):
    m, d = c_all.shape
    n = w_mod.shape[1]
    tn = 1536
    return pl.pallas_call(
        _mod_kernel,
        grid=(n // tn,),
        in_specs=[pl.BlockSpec((m, d), lambda j: (0, 0)),
                  pl.BlockSpec((d, tn), lambda j: (0, j)),
                  pl.BlockSpec((1, tn), lambda j: (0, j))],
        out_specs=pl.BlockSpec((m, tn), lambda j: (0, j)),
        out_shape=jax.ShapeDtypeStruct((m, n), F32),
        compiler_params=pltpu.CompilerParams(dimension_semantics=("arbitrary",), vmem_limit_bytes=VMEM_LIMIT),
        name="modulation",
    )(c_all, w_mod, b_mod.reshape(1, n))


def _head_rms(y, gain, bd):
    hi, lo = _split_bf16(y * y)
    ss = _dot(hi, bd) + _dot(lo, bd)
    return y * lax.rsqrt(ss * (1.0 / HEAD_DIM) + EPS) * gain


def _rope(y, cos, sin):
    n = y.shape[-1]
    lane = lax.broadcasted_iota(jnp.int32, y.shape, 1)
    first_half = (lane % (2 * ROPE_FREQS)) < ROPE_FREQS
    swapped = jnp.where(first_half, pltpu.roll(y, n - ROPE_FREQS, 1), pltpu.roll(y, ROPE_FREQS, 1))
    return y * cos + swapped * sin


def _inproj_latent_kernel(x_ref, sh_ref, sc_ref, g_ref, w_ref, cos_ref, sin_ref, qg_ref, kg_ref, bd_ref,
                          q_ref, k_ref, v_ref, l_ref, gz_ref):
    x = x_ref[0]
    h = x * _rms_scale(x) * g_ref[...] * (1.0 + sc_ref[0]) + sh_ref[0]
    p = _dot(h.astype(BF16), w_ref[...])
    cos, sin = cos_ref[...], sin_ref[...]
    bd = bd_ref[...]
    q = _head_rms(p[:, :D_ATTN], qg_ref[...], bd)
    q = _rope(q, jnp.concatenate([cos] * 4, axis=1), jnp.concatenate([sin] * 4, axis=1)) * (HEAD_DIM ** -0.5 * LOG2_E)
    _store_heads(q_ref, q, N_Q_HEADS)
    k = _head_rms(p[:, D_ATTN:D_ATTN + D_KV], kg_ref[...], bd[:D_KV, :D_KV])
    _store_heads(k_ref, _rope(k, cos, sin), N_KV_HEADS)
    _store_heads(v_ref, p[:, D_ATTN + D_KV:D_ATTN + 2 * D_KV], N_KV_HEADS)
    l_ref[0] = p[:, D_ATTN + 2 * D_KV:D_ATTN + 2 * D_KV + D_LRU]
    gz_ref[0] = jax.nn.gelu(p[:, D_ATTN + 2 * D_KV + D_LRU:])


def _store_heads(ref, y, n_heads):
    for h in range(n_heads):
        ref[0, h] = y[:, h * HEAD_DIM:(h + 1) * HEAD_DIM].astype(BF16)


def _inproj_context_kernel(x_ref, sh_ref, sc_ref, g_ref, w_ref, kg_ref, bd_ref, k_ref, v_ref, l_ref):
    x = x_ref[0]
    h = x * _rms_scale(x) * g_ref[...] * (1.0 + sc_ref[...]) + sh_ref[...]
    p = _dot(h.astype(BF16), w_ref[...])
    _store_heads(k_ref, _head_rms(p[:, :D_KV], kg_ref[...], bd_ref[...]), N_KV_HEADS)
    _store_heads(v_ref, p[:, D_KV:2 * D_KV], N_KV_HEADS)
    l_ref[0] = p[:, 2 * D_KV:]


def _full(shape):
    nd = len(shape)
    return pl.BlockSpec(shape, lambda *_: (0,) * nd)


def _inproj_latent(x, sh, sc, gpre, w_bf, cos, sin, qg, kg, bd, tm):
    B, T, D = x.shape
    tok = lambda n: pl.BlockSpec((1, tm, n), lambda b, i: (b, i, 0))
    vec = pl.BlockSpec((1, 1, D), lambda b, i: (b, 0, 0))
    rope = pl.BlockSpec((tm, LANES), lambda b, i: (i, 0))
    heads = lambda n: pl.BlockSpec((1, n, tm, HEAD_DIM), lambda b, i: (b, 0, i, 0))
    head_shape = lambda n: jax.ShapeDtypeStruct((B, n, T, HEAD_DIM), BF16)
    return pl.pallas_call(
        _inproj_latent_kernel,
        grid=(B, T // tm),
        in_specs=[tok(D), vec, vec, _full((1, D)), _full(w_bf.shape), rope, rope,
                  _full((1, D_ATTN)), _full((1, D_KV)), _full((D_ATTN, D_ATTN))],
        out_specs=[heads(N_Q_HEADS), heads(N_KV_HEADS), heads(N_KV_HEADS), tok(D_LRU), tok(D_LRU)],
        out_shape=[head_shape(N_Q_HEADS), head_shape(N_KV_HEADS), head_shape(N_KV_HEADS),
                   jax.ShapeDtypeStruct((B, T, D_LRU), F32), jax.ShapeDtypeStruct((B, T, D_LRU), F32)],
        compiler_params=pltpu.CompilerParams(dimension_semantics=("arbitrary", "arbitrary"),
                                             vmem_limit_bytes=VMEM_LIMIT),
        name="inproj_latent",
    )(x, sh, sc, gpre, w_bf, cos, sin, qg, kg, bd)


def _inproj_context(ctx, sh, sc, gpre, w_bf, kg, bd, tm):
    B, C, D = ctx.shape
    tok = lambda n: pl.BlockSpec((1, tm, n), lambda b, i: (b, i, 0))
    heads = pl.BlockSpec((1, N_KV_HEADS, tm, HEAD_DIM), lambda b, i: (b, 0, i, 0))
    head_shape = jax.ShapeDtypeStruct((B, N_KV_HEADS, C, HEAD_DIM), BF16)
    return pl.pallas_call(
        _inproj_context_kernel,
        grid=(B, C // tm),
        in_specs=[tok(D), _full((1, D)), _full((1, D)), _full((1, D)), _full(w_bf.shape),
                  _full((1, D_KV)), _full((D_KV, D_KV))],
        out_specs=[heads, heads, tok(D_LRU)],
        out_shape=[head_shape, head_shape, jax.ShapeDtypeStruct((B, C, D_LRU), F32)],
        compiler_params=pltpu.CompilerParams(dimension_semantics=("arbitrary", "arbitrary"),
                                             vmem_limit_bytes=VMEM_LIMIT),
        name="inproj_context",
    )(ctx, sh, sc, gpre, w_bf, kg, bd)


GATE_ROWS = 256
HALO = SUBLANES


def _conv_chunk(src_ref, start, n_rows, cw, cb):
    lo, hi = start - HALO, start + GATE_ROWS + HALO
    parts = []
    if lo < 0:
        parts.append(jnp.zeros((HALO, D_LRU), F32))
        lo = 0
    body_hi = min(hi, n_rows)
    parts.append(src_ref[0, lo:body_hi, :])
    if hi > n_rows:
        parts.append(jnp.zeros((HALO, D_LRU), F32))
    xh = jnp.concatenate(parts, axis=0) if len(parts) > 1 else parts[0]
    out = cb
    for j in range(CONV_W):
        off = HALO + j - CONV_PAD_LEFT
        out = out + xh[off:off + GATE_ROWS, :] * cw[j:j + 1, :]
    return out


def _tile_scan(a, b, carry, row, reverse):
    for sh in (1, 2, 4):
        if reverse:
            a_n, b_n = pltpu.roll(a, SUBLANES - sh, 0), pltpu.roll(b, SUBLANES - sh, 0)
            valid = row < SUBLANES - sh
        else:
            a_n, b_n = pltpu.roll(a, sh, 0), pltpu.roll(b, sh, 0)
            valid = row >= sh
        b = jnp.where(valid, a * b_n + b, b)
        a = jnp.where(valid, a * a_n, a)
    h = a * carry + b
    new_carry = h[0:1, :] if reverse else h[SUBLANES - 1:SUBLANES, :]
    return h, new_carry


def _rglru_kernel(lc_ref, lx_ref, gz_ref, cw_ref, cb_ref, wa_ref, wi_ref, ba_ref, bi_ref, lam_ref, o_ref,
                  a0_ref, b0_ref, a1_ref, b1_ref, *, n_ctx, n_lat):
    cw, cb = cw_ref[...], cb_ref[...]
    a_refs, b_refs = (a0_ref, a1_ref), (b0_ref, b1_ref)
    for seg_ref, seg_rows, seg_off in ((lc_ref, n_ctx, 0), (lx_ref, n_lat, n_ctx)):
        for s in range(0, seg_rows, GATE_ROWS):
            xc = _conv_chunk(seg_ref, s, seg_rows, cw, cb)
            xb = xc.astype(BF16)
            for d in range(2):
                r = jax.nn.sigmoid(_dot(xb, wa_ref[d]) + ba_ref[d])
                i = jax.nn.sigmoid(_dot(xb, wi_ref[d]) + bi_ref[d])
                log_a = -RG_C * r * jax.nn.softplus(-lam_ref[d])
                a = jnp.exp(log_a)
                b = jnp.sqrt(jnp.maximum(1.0 - a * a, 0.0)) * (i * xc)
                a_refs[d][seg_off + s:seg_off + s + GATE_ROWS, :] = a
                b_refs[d][seg_off + s:seg_off + s + GATE_ROWS, :] = b

    ctx_tiles, all_tiles = n_ctx // SUBLANES, (n_ctx + n_lat) // SUBLANES
    row = lax.broadcasted_iota(jnp.int32, (SUBLANES, D_LRU), 0)

    def step(j, carry):
        cf, cr = carry
        tf = pl.multiple_of(j * SUBLANES, SUBLANES)
        jr = jnp.where(j < ctx_tiles, ctx_tiles - 1 - j, all_tiles + ctx_tiles - 1 - j)
        tr = pl.multiple_of(jr * SUBLANES, SUBLANES)
        hf, cf = _tile_scan(a0_ref[pl.ds(tf, SUBLANES), :], b0_ref[pl.ds(tf, SUBLANES), :], cf, row, False)
        hr, cr = _tile_scan(a1_ref[pl.ds(tr, SUBLANES), :], b1_ref[pl.ds(tr, SUBLANES), :], cr, row, True)
        b0_ref[pl.ds(tf, SUBLANES), :] = hf
        b1_ref[pl.ds(tr, SUBLANES), :] = hr
        return cf, cr

    zero = jnp.zeros((1, D_LRU), F32)
    lax.fori_loop(0, all_tiles, step, (zero, zero))
    for s in range(0, n_lat, GATE_ROWS):
        y = b0_ref[n_ctx + s:n_ctx + s + GATE_ROWS, :] + b1_ref[n_ctx + s:n_ctx + s + GATE_ROWS, :]
        o_ref[0, s:s + GATE_ROWS, :] = (y * gz_ref[0, s:s + GATE_ROWS, :]).astype(BF16)


def _rglru(lc, lx, gz, conv_w, conv_b, wa_bd, wi_bd, ba, bi, lam):
    B, C, _ = lc.shape
    T = lx.shape[1]
    seq = lambda n: pl.BlockSpec((1, n, D_LRU), lambda b: (b, 0, 0))
    scratch = [pltpu.VMEM((C + T, D_LRU), F32) for _ in range(4)]
    return pl.pallas_call(
        functools.partial(_rglru_kernel, n_ctx=C, n_lat=T),
        grid=(B,),
        in_specs=[seq(C), seq(T), seq(T), _full((CONV_W, D_LRU)), _full((1, D_LRU)),
                  _full((2, D_LRU, D_LRU)), _full((2, D_LRU, D_LRU)),
                  _full((2, 1, D_LRU)), _full((2, 1, D_LRU)), _full((2, 1, D_LRU))],
        out_specs=seq(T),
        out_shape=jax.ShapeDtypeStruct((B, T, D_LRU), BF16),
        scratch_shapes=scratch,
        compiler_params=pltpu.CompilerParams(dimension_semantics=("arbitrary",), vmem_limit_bytes=VMEM_LIMIT),
        name="rglru",
    )(lc, lx, gz, conv_w, conv_b, wa_bd, wi_bd, ba, bi, lam)


def _attn_kernel(q_ref, k_ref, v_ref, o_ref, *, tq):
    group = N_Q_HEADS // N_KV_HEADS
    for pair in range(N_Q_HEADS // 2):
        g = 2 * pair // group
        q = q_ref[0, 2 * pair:2 * pair + 2].reshape(2 * tq, HEAD_DIM)
        s = _dot_nt(q, k_ref[0, g])
        p = jnp.exp2(s - jnp.max(s, axis=-1, keepdims=True))
        denom = jnp.sum(p, axis=-1, keepdims=True)
        o = _dot(p.astype(BF16), v_ref[0, g]) / denom
        o_ref[0, :, 2 * pair * HEAD_DIM:(2 * pair + 2) * HEAD_DIM] = (
            jnp.concatenate([o[:tq], o[tq:]], axis=1).astype(BF16))


def _attention(q4, k4, v4, tq):
    B, H, T, _ = q4.shape
    S = k4.shape[2]
    kv = pl.BlockSpec((1, N_KV_HEADS, S, HEAD_DIM), lambda b, i: (b, 0, 0, 0))
    return pl.pallas_call(
        functools.partial(_attn_kernel, tq=tq),
        grid=(B, T // tq),
        in_specs=[pl.BlockSpec((1, H, tq, HEAD_DIM), lambda b, i: (b, 0, i, 0)), kv, kv],
        out_specs=pl.BlockSpec((1, tq, H * HEAD_DIM), lambda b, i: (b, i, 0)),
        out_shape=jax.ShapeDtypeStruct((B, T, H * HEAD_DIM), BF16),
        compiler_params=pltpu.CompilerParams(dimension_semantics=("arbitrary", "arbitrary"),
                                             vmem_limit_bytes=VMEM_LIMIT),
        name="attention",
    )(q4, k4, v4)


def _outproj_kernel(attn_ref, lru_ref, x_ref, wout_ref, gpm_ref, g1_ref, gpf_ref, sh_ref, sc_ref, wq_ref,
                    xmid_ref, hx_ref, qp_ref):
    mix = _dot(attn_ref[0], wout_ref[:D_ATTN, :]) + _dot(lru_ref[0], wout_ref[D_ATTN:, :])
    x_mid = x_ref[0] + g1_ref[0] * (mix * _rms_scale(mix) * gpm_ref[...])
    xmid_ref[0] = x_mid
    hx = (x_mid * _rms_scale(x_mid) * gpf_ref[...] * (1.0 + sc_ref[0]) + sh_ref[0]).astype(BF16)
    hx_ref[0] = hx
    qp_ref[0] = _dot(hx, wq_ref[...]).astype(BF16)


def _outproj(attn, lru, x, wout_bf, gpm, g1, gpf, sh2, sc2, wq_bf, tm):
    B, T, D = x.shape
    nq = wq_bf.shape[1]
    tok = lambda n: pl.BlockSpec((1, tm, n), lambda b, i: (b, i, 0))
    vec = pl.BlockSpec((1, 1, D), lambda b, i: (b, 0, 0))
    return pl.pallas_call(
        _outproj_kernel,
        grid=(B, T // tm),
        in_specs=[tok(D_ATTN), tok(D_LRU), tok(D), _full(wout_bf.shape), _full((1, D)), vec, _full((1, D)),
                  vec, vec, _full(wq_bf.shape)],
        out_specs=[tok(D), tok(D), tok(nq)],
        out_shape=[jax.ShapeDtypeStruct((B, T, D), F32), jax.ShapeDtypeStruct((B, T, D), BF16),
                   jax.ShapeDtypeStruct((B, T, nq), BF16)],
        compiler_params=pltpu.CompilerParams(dimension_semantics=("arbitrary", "arbitrary"),
                                             vmem_limit_bytes=VMEM_LIMIT),
        name="outproj",
    )(attn, lru, x, wout_bf, gpm, g1, gpf, sh2, sc2, wq_bf)


NEG_INF = float("-inf")
BIG = 1e9
SORT_GROUP = 4


def _slab_argmax(vs, cs):
    while len(vs) > 1:
        nv, nc = [], []
        for i in range(0, len(vs) - 1, 2):
            take = vs[i + 1] > vs[i]
            nv.append(jnp.where(take, vs[i + 1], vs[i]))
            nc.append(jnp.where(take, cs[i + 1], cs[i]))
        if len(vs) % 2:
            nv.append(vs[-1])
            nc.append(cs[-1])
        vs, cs = nv, nc
    return vs[0], cs[0]


def _extract_topk(vs, cs, k):
    n = vs[0].shape[1]
    slot = lax.broadcasted_iota(jnp.int32, (k, n), 0)
    top_v = jnp.zeros((k, n), F32)
    top_c = jnp.zeros((k, n), F32)
    groups = []
    for i in range(0, len(vs), SORT_GROUP):
        gv, gc = list(vs[i:i + SORT_GROUP]), list(cs[i:i + SORT_GROUP])
        for end in range(len(gv) - 1, 0, -1):
            for a in range(end):
                take = gv[a + 1] > gv[a]
                gv[a], gv[a + 1] = jnp.where(take, gv[a + 1], gv[a]), jnp.where(take, gv[a], gv[a + 1])
                gc[a], gc[a + 1] = jnp.where(take, gc[a + 1], gc[a]), jnp.where(take, gc[a], gc[a + 1])
        groups.append((gv, gc))
    for j in range(k):
        v, c = _slab_argmax([gv[0] for gv, _ in groups], [gc[0] for _, gc in groups])
        m = jnp.max(v, axis=0, keepdims=True)
        cm = jnp.min(jnp.where(v == m, c, BIG), axis=0, keepdims=True)
        top_v = jnp.where(slot == j, m, top_v)
        top_c = jnp.where(slot == j, cm, top_c)
        for gv, gc in groups:
            hit = gc[0] == cm
            for d in range(len(gv) - 1):
                gv[d] = jnp.where(hit, gv[d + 1], gv[d])
                gc[d] = jnp.where(hit, gc[d + 1], gc[d])
            gv[-1] = jnp.where(hit, NEG_INF, gv[-1])
    return top_v, top_c


def _lookup(table, pos):
    out = jnp.zeros(pos.shape, table.dtype)
    for i in range(PEER_TOPK):
        out = jnp.where(pos == i, table[i:i + 1, :], out)
    return out


def _topk_kernel(qp_ref, keys_ref, ia_ref, ib_ref, g_ref, *, tn):
    half = SUBLANES
    sub = lax.broadcasted_iota(jnp.int32, (half, tn), 0).astype(F32)
    key_codes = [sub + float(half * s) for s in range(PEER_NKEYS // half)]
    cand_codes = [sub, sub + float(half)] + [sub + 16.0 * i for i in range(1, half)] + [(sub + float(half)) * 16.0]
    for h in range(PEER_HEADS):
        tops = []
        for p in range(2):
            col = (2 * h + p) * PEER_NKEYS
            s = _dot_nt(keys_ref[2 * h + p], qp_ref[:, col:col + PEER_NKEYS])
            tops.append(_extract_topk([s[half * r:half * (r + 1), :] for r in range(PEER_NKEYS // half)],
                                      key_codes, PEER_TOPK))
        (v1, i1), (v2, i2) = tops
        groups = [v1[0:1, :] + v2[:half, :], v1[0:1, :] + v2[half:, :]]
        groups += [v1[i:i + 1, :] + v2[:half, :] for i in range(1, half)]
        groups += [v1[half:, :] + v2[0:1, :]]
        top_s, top_c = _extract_topk(groups, cand_codes, PEER_TOPK)
        e = jnp.exp(top_s - top_s[0:1, :])
        rows = slice(h * PEER_TOPK, (h + 1) * PEER_TOPK)
        g_ref[rows, :] = e / jnp.sum(e, axis=0, keepdims=True)
        pair = top_c.astype(jnp.int32)
        ia_ref[rows, :] = _lookup(i1, pair >> 4).astype(jnp.int32)
        ib_ref[rows, :] = _lookup(i2, pair & 15).astype(jnp.int32)


def _peer_topk(qp, keys_bf, tn):
    n, nq = qp.shape
    out = pl.BlockSpec((N_SEL, tn), lambda i: (0, i))
    return pl.pallas_call(
        functools.partial(_topk_kernel, tn=tn),
        grid=(n // tn,),
        in_specs=[pl.BlockSpec((tn, nq), lambda i: (i, 0)), _full(keys_bf.shape)],
        out_specs=[out, out, out],
        out_shape=[jax.ShapeDtypeStruct((N_SEL, n), jnp.int32), jax.ShapeDtypeStruct((N_SEL, n), jnp.int32),
                   jax.ShapeDtypeStruct((N_SEL, n), F32)],
        compiler_params=pltpu.CompilerParams(dimension_semantics=("arbitrary",), vmem_limit_bytes=VMEM_LIMIT),
        name="peer_topk",
    )(qp, keys_bf)


EXPERT_CHUNK = 2048
PEER_VMEM_LIMIT = 62 * 1024 * 1024
GROUPS_PER_CHUNK = EXPERT_CHUNK // PEER_NKEYS
TOKEN_UNROLL = 32


def _peer_kernel(hx_ref, ia_ref, ib_ref, g_ref, ut_hbm, v_hbm, xmid_ref, g2_ref, gpost_ref, o_ref,
                 s_ref, ubuf, vbuf, usem, vsem, *, tb, n_chunks):
    srow = tb + SUBLANES

    def group_rows(gidx):
        return pl.ds(pl.multiple_of(gidx * srow, SUBLANES), tb)

    def ut_copy(j, slot):
        return pltpu.make_async_copy(ut_hbm.at[j], ubuf.at[slot], usem.at[slot])

    def v_copy(j, slot):
        return pltpu.make_async_copy(v_hbm.at[j], vbuf.at[slot], vsem.at[slot])

    def prefetch(copy, j, slot):
        @pl.when(j + 1 < n_chunks)
        def _():
            copy(j + 1, 1 - slot).start()

    ut_copy(0, 0).start()

    def score_chunk(j, carry):
        slot = j % 2
        prefetch(ut_copy, j, slot)
        ut_copy(j, slot).wait()
        act = _gelu_tanh(_dot(hx_ref[...], ubuf[slot]))
        for l in range(GROUPS_PER_CHUNK):
            s_ref[group_rows(j * GROUPS_PER_CHUNK + l), :] = act[:, l * PEER_NKEYS:(l + 1) * PEER_NKEYS]
        return carry

    lax.fori_loop(0, n_chunks, score_chunk, 0)
    v_copy(0, 0).start()

    sub = lax.broadcasted_iota(jnp.int32, (PEER_NKEYS, N_SEL), 0)

    def tokens(tt, carry):
        for u in range(TOKEN_UNROLL):
            t = tt * TOKEN_UNROLL + u
            first = jnp.where(sub == ia_ref[t], g_ref[t], 0.0).astype(BF16)
            second = jnp.where(sub == ib_ref[t], 1.0, 0.0).astype(BF16)
            rows = pl.ds(t, PEER_NKEYS, stride=srow)
            s_ref[rows, :] = _dot_nt(first, second) * s_ref[rows, :]
        return carry

    lax.fori_loop(0, tb // TOKEN_UNROLL, tokens, 0)

    def combine_chunk(j, carry):
        slot = j % 2
        prefetch(v_copy, j, slot)
        v_copy(j, slot).wait()
        h = jnp.concatenate([s_ref[group_rows(j * GROUPS_PER_CHUNK + l), :] for l in range(GROUPS_PER_CHUNK)],
                            axis=1)
        part = _dot(h.astype(BF16), vbuf[slot])

        @pl.when(j == 0)
        def _():
            o_ref[...] = part

        @pl.when(j > 0)
        def _():
            o_ref[...] += part

        return carry

    lax.fori_loop(0, n_chunks, combine_chunk, 0)
    f = o_ref[...]
    o_ref[...] = xmid_ref[...] + g2_ref[0] * (f * _rms_scale(f) * gpost_ref[...])


def _peer_dense(hx, ia3, ib3, g3, ut_bf, v_bf, xmid, g2, gpost, tb, tokens_per_batch):
    n, d = hx.shape
    n_chunks = ut_bf.shape[0]
    bpb = tokens_per_batch // tb
    tok = pl.BlockSpec((tb, d), lambda i: (i, 0))
    tok_once = pl.BlockSpec((tb, d), lambda i: (i, 0), pipeline_mode=pl.Buffered(1))
    pick = pl.BlockSpec((tb, 1, N_SEL), lambda i: (i, 0, 0))
    hbm = pl.BlockSpec(memory_space=pl.ANY)
    return pl.pallas_call(
        functools.partial(_peer_kernel, tb=tb, n_chunks=n_chunks),
        grid=(n // tb,),
        in_specs=[tok_once, pick, pick, pick, hbm, hbm,
                  tok_once, pl.BlockSpec((1, 1, d), lambda i: (i // bpb, 0, 0)), _full((1, d))],
        out_specs=tok,
        out_shape=jax.ShapeDtypeStruct((n, d), F32),
        scratch_shapes=[pltpu.VMEM((PEER_NKEYS * (tb + SUBLANES), PEER_NKEYS), F32),
                        pltpu.VMEM((2, d, EXPERT_CHUNK), BF16), pltpu.VMEM((2, EXPERT_CHUNK, d), BF16),
                        pltpu.SemaphoreType.DMA((2,)), pltpu.SemaphoreType.DMA((2,))],
        compiler_params=pltpu.CompilerParams(dimension_semantics=("arbitrary",),
                                             vmem_limit_bytes=PEER_VMEM_LIMIT),
        name="peer_dense",
    )(hx, ia3, ib3, g3, ut_bf, v_bf.reshape(n_chunks, EXPERT_CHUNK, d), xmid, g2, gpost)


def _rope_tables(T):
    t = np.arange(T)
    inv = ROPE_THETA ** (-jnp.arange(ROPE_FREQS, dtype=F32) / ROPE_FREQS)
    ang_r = jnp.asarray(t // GRID_W, F32)[:, None] * inv
    ang_c = jnp.asarray(t % GRID_W, F32)[:, None] * inv
    cos = jnp.concatenate([jnp.cos(ang_r)] * 2 + [jnp.cos(ang_c)] * 2, axis=1)
    sin = jnp.concatenate([-jnp.sin(ang_r), jnp.sin(ang_r), -jnp.sin(ang_c), jnp.sin(ang_c)], axis=1)
    return jnp.concatenate([cos, cos], axis=1), jnp.concatenate([sin, sin], axis=1)


def _block_diag(w):
    n, m, _ = w.shape
    eye = jnp.eye(n, dtype=w.dtype)
    return (eye[:, None, :, None] * w[:, :, None, :]).reshape(n * m, n * m)


def kernel(x, c, ctx, c_ctx, w_mod, b_mod, g_pre_mix, g_post_mix, g_pre_ffn, g_post_ffn, w_in, q_norm_g, k_norm_g, conv_w, conv_b, rg_wa, rg_ba, rg_wi, rg_bi, rg_lambda, w_out, peer_wq, peer_subkeys, peer_u, peer_v):
    B, T, D = x.shape
    C = ctx.shape[1]
    depth = w_mod.shape[0]
    cos, sin = _rope_tables(T)
    head_ones = _block_diag(jnp.ones((N_Q_HEADS, HEAD_DIM, HEAD_DIM), BF16))
    mod_rows = -(-(B + 1) // SUBLANES) * SUBLANES

    for l in range(depth):
        assert l == depth - 1, "context-stream update between layers is not implemented"
        c_all = jnp.zeros((mod_rows, D), F32).at[:B].set(c).at[B].set(c_ctx)
        mod = _modulation(c_all, w_mod[l], b_mod[l])
        mx = mod[:B].reshape(B, N_MOD, 1, D)
        sh1, sc1, g1, sh2, sc2, g2 = (mx[:, i] for i in range(N_MOD))
        mc = mod[B].reshape(N_MOD, 1, D)

        w_bf = w_in[l].astype(BF16)
        qg = jnp.tile(q_norm_g[l], N_Q_HEADS).reshape(1, D_ATTN)
        kg = jnp.tile(k_norm_g[l], N_KV_HEADS).reshape(1, D_KV)
        gpre = g_pre_mix[l].reshape(1, D)
        q, kx, vx, lx, gz = _inproj_latent(x, sh1, sc1, gpre, w_bf, cos, sin, qg, kg, head_ones, tm=512)
        kc, vc, lc = _inproj_context(ctx, mc[0], mc[1], gpre, w_bf[:, D_ATTN:D_ATTN + 2 * D_KV + D_LRU],
                                     kg, head_ones[:D_KV, :D_KV], tm=C)

        lru = _rglru(lc, lx, gz, conv_w[l], conv_b[l].reshape(1, D_LRU),
                     jnp.stack([_block_diag(rg_wa[l, d]) for d in range(2)]).astype(BF16),
                     jnp.stack([_block_diag(rg_wi[l, d]) for d in range(2)]).astype(BF16),
                     rg_ba[l].reshape(2, 1, D_LRU), rg_bi[l].reshape(2, 1, D_LRU), rg_lambda[l].reshape(2, 1, D_LRU))

        attn = _attention(q, jnp.concatenate([kc, kx], axis=2), jnp.concatenate([vc, vx], axis=2), tq=256)

        x_mid, hx, qp = _outproj(attn, lru, x, w_out[l].astype(BF16), g_post_mix[l].reshape(1, D), g1,
                                 g_pre_ffn[l].reshape(1, D), sh2, sc2, peer_wq[l].astype(BF16), tm=256)

        n = B * T
        keys_bf = peer_subkeys[l].reshape(2 * PEER_HEADS, PEER_NKEYS, -1).astype(BF16)
        ia, ib, gate = _peer_topk(qp.reshape(n, -1), keys_bf, tn=256)
        picks = lambda a: a.T.reshape(n, 1, N_SEL)
        ut = peer_u[l].astype(BF16).reshape(-1, EXPERT_CHUNK, D).transpose(0, 2, 1)
        x = _peer_dense(hx.reshape(n, D), picks(ia), picks(ib), picks(gate), ut,
                        peer_v[l].astype(BF16), x_mid.reshape(n, D), g2, g_post_ffn[l].reshape(1, D),
                        tb=min(512, T), tokens_per_batch=T).reshape(B, T, D)
    return x
```

```python
import functools

import jax
import jax.numpy as jnp
import numpy as np
from jax import lax
from jax.experimental import pallas as pl
from jax.experimental.pallas import tpu as pltpu

F32 = jnp.float32
BF16 = jnp.bfloat16

HEAD_DIM = 64
N_Q_HEADS = 8
N_KV_HEADS = 2
D_ATTN = 512
D_KV = 128
D_LRU = 512
LRU_BLOCKS = 8
CONV_W = 4
CONV_PAD_LEFT = 2
RG_C = 8.0
ROPE_THETA = 10000.0
ROPE_FREQS = 16
GRID_W = 64
N_MOD = 6
EPS = 1e-6
PEER_HEADS = 8
PEER_NKEYS = 128
PEER_TOPK = 16
N_SEL = PEER_HEADS * PEER_TOPK

SUBLANES = 8
LANES = 128
VMEM_LIMIT = 56 * 1024 * 1024

LOG2_E = 1.4426950408889634
NT_DIMS = (((1,), (1,)), ((), ()))


def _dot(a, b):
    return jnp.dot(a, b, preferred_element_type=F32)


def _dot_nt(a, b):
    return lax.dot_general(a, b, NT_DIMS, preferred_element_type=F32)


def _split_bf16(x):
    hi = x.astype(BF16)
    return hi, (x - hi.astype(F32)).astype(BF16)


def _gelu_tanh(x):
    k = -2.0 * LOG2_E * (2.0 / np.pi) ** 0.5
    return x / (1.0 + jnp.exp2(x * (k + (k * 0.044715) * (x * x))))


def _rms_scale(x):
    return lax.rsqrt(jnp.mean(x * x, axis=-1, keepdims=True) + EPS)


def _mod_kernel(c_ref, w_ref, b_ref, o_ref):
    s = jax.nn.silu(c_ref[...])
    s_hi, s_lo = _split_bf16(s)
    w_hi, w_lo = _split_bf16(w_ref[...])
    o_ref[...] = _dot(s_hi, w_hi) + _dot(s_lo, w_hi) + _dot(s_hi, w_lo) + b_ref[...]


def _modulation(c_all, w_mod, b_mod):
    m, d = c_all.shape
    n = w_mod.shape[1]
    tn = 1536
    return pl.pallas_call(
        _mod_kernel,
        grid=(n // tn,),
        in_specs=[pl.BlockSpec((m, d), lambda j: (0, 0)),
                  pl.BlockSpec((d, tn), lambda j: (0, j)),
                  pl.BlockSpec((1, tn), lambda j: (0, j))],
        out_specs=pl.BlockSpec((m, tn), lambda j: (0, j)),
        out_shape=jax.ShapeDtypeStruct((m, n), F32),
        compiler_params=pltpu.CompilerParams(dimension_semantics=("arbitrary",), vmem_limit_bytes=VMEM_LIMIT),
        name="modulation",
    )(c_all, w_mod, b_mod.reshape(1, n))


def _head_rms(y, gain, bd):
    hi, lo = _split_bf16(y * y)
    ss = _dot(hi, bd) + _dot(lo, bd)
    return y * lax.rsqrt(ss * (1.0 / HEAD_DIM) + EPS) * gain


def _rope(y, cos, sin):
    n = y.shape[-1]
    lane = lax.broadcasted_iota(jnp.int32, y.shape, 1)
    first_half = (lane % (2 * ROPE_FREQS)) < ROPE_FREQS
    swapped = jnp.where(first_half, pltpu.roll(y, n - ROPE_FREQS, 1), pltpu.roll(y, ROPE_FREQS, 1))
    return y * cos + swapped * sin


def _inproj_latent_kernel(x_ref, sh_ref, sc_ref, g_ref, w_ref, cos_ref, sin_ref, qg_ref, kg_ref, bd_ref,
                          q_ref, k_ref, v_ref, l_ref, gz_ref):
    x = x_ref[0]
    h = x * _rms_scale(x) * g_ref[...] * (1.0 + sc_ref[0]) + sh_ref[0]
    p = _dot(h.astype(BF16), w_ref[...])
    cos, sin = cos_ref[...], sin_ref[...]
    bd = bd_ref[...]
    q = _head_rms(p[:, :D_ATTN], qg_ref[...], bd)
    q = _rope(q, jnp.concatenate([cos] * 4, axis=1), jnp.concatenate([sin] * 4, axis=1)) * (HEAD_DIM ** -0.5 * LOG2_E)
    _store_heads(q_ref, q, N_Q_HEADS)
    k = _head_rms(p[:, D_ATTN:D_ATTN + D_KV], kg_ref[...], bd[:D_KV, :D_KV])
    _store_heads(k_ref, _rope(k, cos, sin), N_KV_HEADS)
    _store_heads(v_ref, p[:, D_ATTN + D_KV:D_ATTN + 2 * D_KV], N_KV_HEADS)
    l_ref[0] = p[:, D_ATTN + 2 * D_KV:D_ATTN + 2 * D_KV + D_LRU]
    gz_ref[0] = jax.nn.gelu(p[:, D_ATTN + 2 * D_KV + D_LRU:])


def _store_heads(ref, y, n_heads):
    for h in range(n_heads):
        ref[0, h] = y[:, h * HEAD_DIM:(h + 1) * HEAD_DIM].astype(BF16)


def _inproj_context_kernel(x_ref, sh_ref, sc_ref, g_ref, w_ref, kg_ref, bd_ref, k_ref, v_ref, l_ref):
    x = x_ref[0]
    h = x * _rms_scale(x) * g_ref[...] * (1.0 + sc_ref[...]) + sh_ref[...]
    p = _dot(h.astype(BF16), w_ref[...])
    _store_heads(k_ref, _head_rms(p[:, :D_KV], kg_ref[...], bd_ref[...]), N_KV_HEADS)
    _store_heads(v_ref, p[:, D_KV:2 * D_KV], N_KV_HEADS)
    l_ref[0] = p[:, 2 * D_KV:]


def _full(shape):
    nd = len(shape)
    return pl.BlockSpec(shape, lambda *_: (0,) * nd)


def _inproj_latent(x, sh, sc, gpre, w_bf, cos, sin, qg, kg, bd, tm):
    B, T, D = x.shape
    tok = lambda n: pl.BlockSpec((1, tm, n), lambda b, i: (b, i, 0))
    vec = pl.BlockSpec((1, 1, D), lambda b, i: (b, 0, 0))
    rope = pl.BlockSpec((tm, LANES), lambda b, i: (i, 0))
    heads = lambda n: pl.BlockSpec((1, n, tm, HEAD_DIM), lambda b, i: (b, 0, i, 0))
    head_shape = lambda n: jax.ShapeDtypeStruct((B, n, T, HEAD_DIM), BF16)
    return pl.pallas_call(
        _inproj_latent_kernel,
        grid=(B, T // tm),
        in_specs=[tok(D), vec, vec, _full((1, D)), _full(w_bf.shape), rope, rope,
                  _full((1, D_ATTN)), _full((1, D_KV)), _full((D_ATTN, D_ATTN))],
        out_specs=[heads(N_Q_HEADS), heads(N_KV_HEADS), heads(N_KV_HEADS), tok(D_LRU), tok(D_LRU)],
        out_shape=[head_shape(N_Q_HEADS), head_shape(N_KV_HEADS), head_shape(N_KV_HEADS),
                   jax.ShapeDtypeStruct((B, T, D_LRU), F32), jax.ShapeDtypeStruct((B, T, D_LRU), F32)],
        compiler_params=pltpu.CompilerParams(dimension_semantics=("arbitrary", "arbitrary"),
                                             vmem_limit_bytes=VMEM_LIMIT),
        name="inproj_latent",
    )(x, sh, sc, gpre, w_bf, cos, sin, qg, kg, bd)


def _inproj_context(ctx, sh, sc, gpre, w_bf, kg, bd, tm):
    B, C, D = ctx.shape
    tok = lambda n: pl.BlockSpec((1, tm, n), lambda b, i: (b, i, 0))
    heads = pl.BlockSpec((1, N_KV_HEADS, tm, HEAD_DIM), lambda b, i: (b, 0, i, 0))
    head_shape = jax.ShapeDtypeStruct((B, N_KV_HEADS, C, HEAD_DIM), BF16)
    return pl.pallas_call(
        _inproj_context_kernel,
        grid=(B, C // tm),
        in_specs=[tok(D), _full((1, D)), _full((1, D)), _full((1, D)), _full(w_bf.shape),
                  _full((1, D_KV)), _full((D_KV, D_KV))],
        out_specs=[heads, heads, tok(D_LRU)],
        out_shape=[head_shape, head_shape, jax.ShapeDtypeStruct((B, C, D_LRU), F32)],
        compiler_params=pltpu.CompilerParams(dimension_semantics=("arbitrary", "arbitrary"),
                                             vmem_limit_bytes=VMEM_LIMIT),
        name="inproj_context",
    )(ctx, sh, sc, gpre, w_bf, kg, bd)


GATE_ROWS = 256
HALO = SUBLANES


def _conv_chunk(src_ref, start, n_rows, cw, cb):
    lo, hi = start - HALO, start + GATE_ROWS + HALO
    parts = []
    if lo < 0:
        parts.append(jnp.zeros((HALO, D_LRU), F32))
        lo = 0
    body_hi = min(hi, n_rows)
    parts.append(src_ref[0, lo:body_hi, :])
    if hi > n_rows:
        parts.append(jnp.zeros((HALO, D_LRU), F32))
    xh = jnp.concatenate(parts, axis=0) if len(parts) > 1 else parts[0]
    out = cb
    for j in range(CONV_W):
        off = HALO + j - CONV_PAD_LEFT
        out = out + xh[off:off + GATE_ROWS, :] * cw[j:j + 1, :]
    return out


def _tile_scan(a, b, carry, row, reverse):
    for sh in (1, 2, 4):
        if reverse:
            a_n, b_n = pltpu.roll(a, SUBLANES - sh, 0), pltpu.roll(b, SUBLANES - sh, 0)
            valid = row < SUBLANES - sh
        else:
            a_n, b_n = pltpu.roll(a, sh, 0), pltpu.roll(b, sh, 0)
            valid = row >= sh
        b = jnp.where(valid, a * b_n + b, b)
        a = jnp.where(valid, a * a_n, a)
    h = a * carry + b
    new_carry = h[0:1, :] if reverse else h[SUBLANES - 1:SUBLANES, :]
    return h, new_carry


def _rglru_kernel(lc_ref, lx_ref, gz_ref, cw_ref, cb_ref, wa_ref, wi_ref, ba_ref, bi_ref, lam_ref, o_ref,
                  a0_ref, b0_ref, a1_ref, b1_ref, *, n_ctx, n_lat):
    cw, cb = cw_ref[...], cb_ref[...]
    a_refs, b_refs = (a0_ref, a1_ref), (b0_ref, b1_ref)
    for seg_ref, seg_rows, seg_off in ((lc_ref, n_ctx, 0), (lx_ref, n_lat, n_ctx)):
        for s in range(0, seg_rows, GATE_ROWS):
            xc = _conv_chunk(seg_ref, s, seg_rows, cw, cb)
            xb = xc.astype(BF16)
            for d in range(2):
                r = jax.nn.sigmoid(_dot(xb, wa_ref[d]) + ba_ref[d])
                i = jax.nn.sigmoid(_dot(xb, wi_ref[d]) + bi_ref[d])
                log_a = -RG_C * r * jax.nn.softplus(-lam_ref[d])
                a = jnp.exp(log_a)
                b = jnp.sqrt(jnp.maximum(1.0 - a * a, 0.0)) * (i * xc)
                a_refs[d][seg_off + s:seg_off + s + GATE_ROWS, :] = a
                b_refs[d][seg_off + s:seg_off + s + GATE_ROWS, :] = b

    ctx_tiles, all_tiles = n_ctx // SUBLANES, (n_ctx + n_lat) // SUBLANES
    row = lax.broadcasted_iota(jnp.int32, (SUBLANES, D_LRU), 0)

    def step(j, carry):
        cf, cr = carry
        tf = pl.multiple_of(j * SUBLANES, SUBLANES)
        jr = jnp.where(j < ctx_tiles, ctx_tiles - 1 - j, all_tiles + ctx_tiles - 1 - j)
        tr = pl.multiple_of(jr * SUBLANES, SUBLANES)
        hf, cf = _tile_scan(a0_ref[pl.ds(tf, SUBLANES), :], b0_ref[pl.ds(tf, SUBLANES), :], cf, row, False)
        hr, cr = _tile_scan(a1_ref[pl.ds(tr, SUBLANES), :], b1_ref[pl.ds(tr, SUBLANES), :], cr, row, True)
        b0_ref[pl.ds(tf, SUBLANES), :] = hf
        b1_ref[pl.ds(tr, SUBLANES), :] = hr
        return cf, cr

    zero = jnp.zeros((1, D_LRU), F32)
    lax.fori_loop(0, all_tiles, step, (zero, zero))
    for s in range(0, n_lat, GATE_ROWS):
        y = b0_ref[n_ctx + s:n_ctx + s + GATE_ROWS, :] + b1_ref[n_ctx + s:n_ctx + s + GATE_ROWS, :]
        o_ref[0, s:s + GATE_ROWS, :] = (y * gz_ref[0, s:s + GATE_ROWS, :]).astype(BF16)


def _rglru(lc, lx, gz, conv_w, conv_b, wa_bd, wi_bd, ba, bi, lam):
    B, C, _ = lc.shape
    T = lx.shape[1]
    seq = lambda n: pl.BlockSpec((1, n, D_LRU), lambda b: (b, 0, 0))
    scratch = [pltpu.VMEM((C + T, D_LRU), F32) for _ in range(4)]
    return pl.pallas_call(
        functools.partial(_rglru_kernel, n_ctx=C, n_lat=T),
        grid=(B,),
        in_specs=[seq(C), seq(T), seq(T), _full((CONV_W, D_LRU)), _full((1, D_LRU)),
                  _full((2, D_LRU, D_LRU)), _full((2, D_LRU, D_LRU)),
                  _full((2, 1, D_LRU)), _full((2, 1, D_LRU)), _full((2, 1, D_LRU))],
        out_specs=seq(T),
        out_shape=jax.ShapeDtypeStruct((B, T, D_LRU), BF16),
        scratch_shapes=scratch,
        compiler_params=pltpu.CompilerParams(dimension_semantics=("arbitrary",), vmem_limit_bytes=VMEM_LIMIT),
        name="rglru",
    )(lc, lx, gz, conv_w, conv_b, wa_bd, wi_bd, ba, bi, lam)


def _attn_kernel(q_ref, k_ref, v_ref, o_ref, *, tq):
    group = N_Q_HEADS // N_KV_HEADS
    for pair in range(N_Q_HEADS // 2):
        g = 2 * pair // group
        q = q_ref[0, 2 * pair:2 * pair + 2].reshape(2 * tq, HEAD_DIM)
        s = _dot_nt(q, k_ref[0, g])
        p = jnp.exp2(s - jnp.max(s, axis=-1, keepdims=True))
        denom = jnp.sum(p, axis=-1, keepdims=True)
        o = _dot(p.astype(BF16), v_ref[0, g]) / denom
        o_ref[0, :, 2 * pair * HEAD_DIM:(2 * pair + 2) * HEAD_DIM] = (
            jnp.concatenate([o[:tq], o[tq:]], axis=1).astype(BF16))


def _attention(q4, k4, v4, tq):
    B, H, T, _ = q4.shape
    S = k4.shape[2]
    kv = pl.BlockSpec((1, N_KV_HEADS, S, HEAD_DIM), lambda b, i: (b, 0, 0, 0))
    return pl.pallas_call(
        functools.partial(_attn_kernel, tq=tq),
        grid=(B, T // tq),
        in_specs=[pl.BlockSpec((1, H, tq, HEAD_DIM), lambda b, i: (b, 0, i, 0)), kv, kv],
        out_specs=pl.BlockSpec((1, tq, H * HEAD_DIM), lambda b, i: (b, i, 0)),
        out_shape=jax.ShapeDtypeStruct((B, T, H * HEAD_DIM), BF16),
        compiler_params=pltpu.CompilerParams(dimension_semantics=("arbitrary", "arbitrary"),
                                             vmem_limit_bytes=VMEM_LIMIT),
        name="attention",
    )(q4, k4, v4)


def _outproj_kernel(attn_ref, lru_ref, x_ref, wout_ref, gpm_ref, g1_ref, gpf_ref, sh_ref, sc_ref, wq_ref,
                    xmid_ref, hx_ref, qp_ref):
    mix = _dot(attn_ref[0], wout_ref[:D_ATTN, :]) + _dot(lru_ref[0], wout_ref[D_ATTN:, :])
    x_mid = x_ref[0] + g1_ref[0] * (mix * _rms_scale(mix) * gpm_ref[...])
    xmid_ref[0] = x_mid
    hx = (x_mid * _rms_scale(x_mid) * gpf_ref[...] * (1.0 + sc_ref[0]) + sh_ref[0]).astype(BF16)
    hx_ref[0] = hx
    qp_ref[0] = _dot(hx, wq_ref[...]).astype(BF16)


def _outproj(attn, lru, x, wout_bf, gpm, g1, gpf, sh2, sc2, wq_bf, tm):
    B, T, D = x.shape
    nq = wq_bf.shape[1]
    tok = lambda n: pl.BlockSpec((1, tm, n), lambda b, i: (b, i, 0))
    vec = pl.BlockSpec((1, 1, D), lambda b, i: (b, 0, 0))
    return pl.pallas_call(
        _outproj_kernel,
        grid=(B, T // tm),
        in_specs=[tok(D_ATTN), tok(D_LRU), tok(D), _full(wout_bf.shape), _full((1, D)), vec, _full((1, D)),
                  vec, vec, _full(wq_bf.shape)],
        out_specs=[tok(D), tok(D), tok(nq)],
        out_shape=[jax.ShapeDtypeStruct((B, T, D), F32), jax.ShapeDtypeStruct((B, T, D), BF16),
                   jax.ShapeDtypeStruct((B, T, nq), BF16)],
        compiler_params=pltpu.CompilerParams(dimension_semantics=("arbitrary", "arbitrary"),
                                             vmem_limit_bytes=VMEM_LIMIT),
        name="outproj",
    )(attn, lru, x, wout_bf, gpm, g1, gpf, sh2, sc2, wq_bf)


NEG_INF = float("-inf")
BIG = 1e9
SORT_GROUP = 4


def _slab_argmax(vs, cs):
    while len(vs) > 1:
        nv, nc = [], []
        for i in range(0, len(vs) - 1, 2):
            take = vs[i + 1] > vs[i]
            nv.append(jnp.where(take, vs[i + 1], vs[i]))
            nc.append(jnp.where(take, cs[i + 1], cs[i]))
        if len(vs) % 2:
            nv.append(vs[-1])
            nc.append(cs[-1])
        vs, cs = nv, nc
    return vs[0], cs[0]


def _extract_topk(vs, cs, k):
    n = vs[0].shape[1]
    slot = lax.broadcasted_iota(jnp.int32, (k, n), 0)
    top_v = jnp.zeros((k, n), F32)
    top_c = jnp.zeros((k, n), F32)
    groups = []
    for i in range(0, len(vs), SORT_GROUP):
        gv, gc = list(vs[i:i + SORT_GROUP]), list(cs[i:i + SORT_GROUP])
        for end in range(len(gv) - 1, 0, -1):
            for a in range(end):
                take = gv[a + 1] > gv[a]
                gv[a], gv[a + 1] = jnp.where(take, gv[a + 1], gv[a]), jnp.where(take, gv[a], gv[a + 1])
                gc[a], gc[a + 1] = jnp.where(take, gc[a + 1], gc[a]), jnp.where(take, gc[a], gc[a + 1])
        groups.append((gv, gc))
    for j in range(k):
        v, c = _slab_argmax([gv[0] for gv, _ in groups], [gc[0] for _, gc in groups])
        m = jnp.max(v, axis=0, keepdims=True)
        cm = jnp.min(jnp.where(v == m, c, BIG), axis=0, keepdims=True)
        top_v = jnp.where(slot == j, m, top_v)
        top_c = jnp.where(slot == j, cm, top_c)
        for gv, gc in groups:
            hit = gc[0] == cm
            for d in range(len(gv) - 1):
                gv[d] = jnp.where(hit, gv[d + 1], gv[d])
                gc[d] = jnp.where(hit, gc[d + 1], gc[d])
            gv[-1] = jnp.where(hit, NEG_INF, gv[-1])
    return top_v, top_c


def _lookup(table, pos):
    out = jnp.zeros(pos.shape, table.dtype)
    for i in range(PEER_TOPK):
        out = jnp.where(pos == i, table[i:i + 1, :], out)
    return out


def _topk_kernel(qp_ref, keys_ref, ia_ref, ib_ref, g_ref, *, tn):
    half = SUBLANES
    sub = lax.broadcasted_iota(jnp.int32, (half, tn), 0).astype(F32)
    key_codes = [sub + float(half * s) for s in range(PEER_NKEYS // half)]
    cand_codes = [sub, sub + float(half)] + [sub + 16.0 * i for i in range(1, half)] + [(sub + float(half)) * 16.0]
    for h in range(PEER_HEADS):
        tops = []
        for p in range(2):
            col = (2 * h + p) * PEER_NKEYS
            s = _dot_nt(keys_ref[2 * h + p], qp_ref[:, col:col + PEER_NKEYS])
            tops.append(_extract_topk([s[half * r:half * (r + 1), :] for r in range(PEER_NKEYS // half)],
                                      key_codes, PEER_TOPK))
        (v1, i1), (v2, i2) = tops
        groups = [v1[0:1, :] + v2[:half, :], v1[0:1, :] + v2[half:, :]]
        groups += [v1[i:i + 1, :] + v2[:half, :] for i in range(1, half)]
        groups += [v1[half:, :] + v2[0:1, :]]
        top_s, top_c = _extract_topk(groups, cand_codes, PEER_TOPK)
        e = jnp.exp(top_s - top_s[0:1, :])
        rows = slice(h * PEER_TOPK, (h + 1) * PEER_TOPK)
        g_ref[rows, :] = e / jnp.sum(e, axis=0, keepdims=True)
        pair = top_c.astype(jnp.int32)
        ia_ref[rows, :] = _lookup(i1, pair >> 4).astype(jnp.int32)
        ib_ref[rows, :] = _lookup(i2, pair & 15).astype(jnp.int32)


def _peer_topk(qp, keys_bf, tn):
    n, nq = qp.shape
    out = pl.BlockSpec((N_SEL, tn), lambda i: (0, i))
    return pl.pallas_call(
        functools.partial(_topk_kernel, tn=tn),
        grid=(n // tn,),
        in_specs=[pl.BlockSpec((tn, nq), lambda i: (i, 0)), _full(keys_bf.shape)],
        out_specs=[out, out, out],
        out_shape=[jax.ShapeDtypeStruct((N_SEL, n), jnp.int32), jax.ShapeDtypeStruct((N_SEL, n), jnp.int32),
                   jax.ShapeDtypeStruct((N_SEL, n), F32)],
        compiler_params=pltpu.CompilerParams(dimension_semantics=("arbitrary",), vmem_limit_bytes=VMEM_LIMIT),
        name="peer_topk",
    )(qp, keys_bf)


EXPERT_CHUNK = 2048
PEER_VMEM_LIMIT = 62 * 1024 * 1024
GROUPS_PER_CHUNK = EXPERT_CHUNK // PEER_NKEYS
TOKEN_UNROLL = 32
GROUP_PAD = 4


def _peer_kernel(hx_ref, ia_ref, ib_ref, g_ref, ut_ref, v_ref, xmid_ref, g2_ref, gpost_ref, o_ref,
                 s_ref, *, tb, n_chunks):
    j = pl.program_id(1)
    srow = tb + GROUP_PAD

    def group_rows(chunk, l):
        return pl.ds(pl.multiple_of(chunk * (GROUPS_PER_CHUNK * srow), SUBLANES) + l * srow, tb)

    @pl.when(j < n_chunks)
    def _scores():
        act = _gelu_tanh(_dot(hx_ref[...], ut_ref[0]))
        for l in range(GROUPS_PER_CHUNK):
            s_ref[group_rows(j, l), :] = act[:, l * PEER_NKEYS:(l + 1) * PEER_NKEYS]

    @pl.when(j == n_chunks)
    def _select():
        sub = lax.broadcasted_iota(jnp.int32, (PEER_NKEYS, N_SEL), 0)

        def tokens(tt, carry):
            for u in range(TOKEN_UNROLL):
                t = tt * TOKEN_UNROLL + u
                first = jnp.where(sub == ia_ref[t], g_ref[t], 0.0).astype(BF16)
                second = jnp.where(sub == ib_ref[t], 1.0, 0.0).astype(BF16)
                rows = pl.ds(t, PEER_NKEYS, stride=srow)
                s_ref[rows, :] = _dot_nt(first, second) * s_ref[rows, :]
            return carry

        lax.fori_loop(0, tb // TOKEN_UNROLL, tokens, 0)

    @pl.when(j >= n_chunks)
    def _combine():
        jj = j - n_chunks
        h = jnp.concatenate([s_ref[group_rows(jj, l), :] for l in range(GROUPS_PER_CHUNK)], axis=1)
        part = _dot(h.astype(BF16), v_ref[...])

        @pl.when(jj == 0)
        def _():
            o_ref[...] = part

        @pl.when(jj > 0)
        def _():
            o_ref[...] += part

    @pl.when(j == 2 * n_chunks - 1)
    def _finish():
        f = o_ref[...]
        o_ref[...] = xmid_ref[...] + g2_ref[0] * (f * _rms_scale(f) * gpost_ref[...])


def _peer_dense(hx, ia3, ib3, g3, ut_bf, v_bf, xmid, g2, gpost, tb, tokens_per_batch):
    n, d = hx.shape
    n_chunks = ut_bf.shape[0]
    bpb = tokens_per_batch // tb
    tok = pl.BlockSpec((tb, d), lambda i, j: (i, 0))
    tok_once = pl.BlockSpec((tb, d), lambda i, j: (i, 0), pipeline_mode=pl.Buffered(1))
    pick = pl.BlockSpec((tb, 1, N_SEL), lambda i, j: (i, 0, 0))
    return pl.pallas_call(
        functools.partial(_peer_kernel, tb=tb, n_chunks=n_chunks),
        grid=(n // tb, 2 * n_chunks),
        in_specs=[tok_once, pick, pick, pick,
                  pl.BlockSpec((1, d, EXPERT_CHUNK), lambda i, j: (jnp.minimum(j, n_chunks - 1), 0, 0)),
                  pl.BlockSpec((EXPERT_CHUNK, d), lambda i, j: (jnp.maximum(j - n_chunks, 0), 0)),
                  tok_once, pl.BlockSpec((1, 1, d), lambda i, j: (i // bpb, 0, 0)), _full((1, d))],
        out_specs=tok,
        out_shape=jax.ShapeDtypeStruct((n, d), F32),
        scratch_shapes=[pltpu.VMEM((PEER_NKEYS * (tb + GROUP_PAD), PEER_NKEYS), F32)],
        compiler_params=pltpu.CompilerParams(dimension_semantics=("arbitrary", "arbitrary"),
                                             vmem_limit_bytes=PEER_VMEM_LIMIT),
        name="peer_dense",
    )(hx, ia3, ib3, g3, ut_bf, v_bf, xmid, g2, gpost)


def _rope_tables(T):
    t = np.arange(T)
    inv = ROPE_THETA ** (-jnp.arange(ROPE_FREQS, dtype=F32) / ROPE_FREQS)
    ang_r = jnp.asarray(t // GRID_W, F32)[:, None] * inv
    ang_c = jnp.asarray(t % GRID_W, F32)[:, None] * inv
    cos = jnp.concatenate([jnp.cos(ang_r)] * 2 + [jnp.cos(ang_c)] * 2, axis=1)
    sin = jnp.concatenate([-jnp.sin(ang_r), jnp.sin(ang_r), -jnp.sin(ang_c), jnp.sin(ang_c)], axis=1)
    return jnp.concatenate([cos, cos], axis=1), jnp.concatenate([sin, sin], axis=1)


def _block_diag(w):
    n, m, _ = w.shape
    eye = jnp.eye(n, dtype=w.dtype)
    return (eye[:, None, :, None] * w[:, :, None, :]).reshape(n * m, n * m)


def kernel(x, c, ctx, c_ctx, w_mod, b_mod, g_pre_mix, g_post_mix, g_pre_ffn, g_post_ffn, w_in, q_norm_g, k_norm_g, conv_w, conv_b, rg_wa, rg_ba, rg_wi, rg_bi, rg_lambda, w_out, peer_wq, peer_subkeys, peer_u, peer_v):
    B, T, D = x.shape
    C = ctx.shape[1]
    depth = w_mod.shape[0]
    cos, sin = _rope_tables(T)
    head_ones = _block_diag(jnp.ones((N_Q_HEADS, HEAD_DIM, HEAD_DIM), BF16))
    mod_rows = -(-(B + 1) // SUBLANES) * SUBLANES

    for l in range(depth):
        assert l == depth - 1, "context-stream update between layers is not implemented"
        c_all = jnp.zeros((mod_rows, D), F32).at[:B].set(c).at[B].set(c_ctx)
        mod = _modulation(c_all, w_mod[l], b_mod[l])
        mx = mod[:B].reshape(B, N_MOD, 1, D)
        sh1, sc1, g1, sh2, sc2, g2 = (mx[:, i] for i in range(N_MOD))
        mc = mod[B].reshape(N_MOD, 1, D)

        w_bf = w_in[l].astype(BF16)
        qg = jnp.tile(q_norm_g[l], N_Q_HEADS).reshape(1, D_ATTN)
        kg = jnp.tile(k_norm_g[l], N_KV_HEADS).reshape(1, D_KV)
        gpre = g_pre_mix[l].reshape(1, D)
        q, kx, vx, lx, gz = _inproj_latent(x, sh1, sc1, gpre, w_bf, cos, sin, qg, kg, head_ones, tm=512)
        kc, vc, lc = _inproj_context(ctx, mc[0], mc[1], gpre, w_bf[:, D_ATTN:D_ATTN + 2 * D_KV + D_LRU],
                                     kg, head_ones[:D_KV, :D_KV], tm=C)

        lru = _rglru(lc, lx, gz, conv_w[l], conv_b[l].reshape(1, D_LRU),
                     jnp.stack([_block_diag(rg_wa[l, d]) for d in range(2)]).astype(BF16),
                     jnp.stack([_block_diag(rg_wi[l, d]) for d in range(2)]).astype(BF16),
                     rg_ba[l].reshape(2, 1, D_LRU), rg_bi[l].reshape(2, 1, D_LRU), rg_lambda[l].reshape(2, 1, D_LRU))

        attn = _attention(q, jnp.concatenate([kc, kx], axis=2), jnp.concatenate([vc, vx], axis=2), tq=256)

        x_mid, hx, qp = _outproj(attn, lru, x, w_out[l].astype(BF16), g_post_mix[l].reshape(1, D), g1,
                                 g_pre_ffn[l].reshape(1, D), sh2, sc2, peer_wq[l].astype(BF16), tm=256)

        n = B * T
        keys_bf = peer_subkeys[l].reshape(2 * PEER_HEADS, PEER_NKEYS, -1).astype(BF16)
        ia, ib, gate = _peer_topk(qp.reshape(n, -1), keys_bf, tn=256)
        picks = lambda a: a.T.reshape(n, 1, N_SEL)
        ut = peer_u[l].astype(BF16).reshape(-1, EXPERT_CHUNK, D).transpose(0, 2, 1)
        x = _peer_dense(hx.reshape(n, D), picks(ia), picks(ib), picks(gate), ut,
                        peer_v[l].astype(BF16), x_mid.reshape(n, D), g2, g_post_ffn[l].reshape(1, D),
                        tb=min(512, T), tokens_per_batch=T).reshape(B, T, D)
    return x
```

```python
import functools

import jax
import jax.numpy as jnp
import numpy as np
from jax import lax
from jax.experimental import pallas as pl
from jax.experimental.pallas import tpu as pltpu

F32 = jnp.float32
BF16 = jnp.bfloat16

HEAD_DIM = 64
N_Q_HEADS = 8
N_KV_HEADS = 2
D_ATTN = 512
D_KV = 128
D_LRU = 512
LRU_BLOCKS = 8
CONV_W = 4
CONV_PAD_LEFT = 2
RG_C = 8.0
ROPE_THETA = 10000.0
ROPE_FREQS = 16
GRID_W = 64
N_MOD = 6
EPS = 1e-6
PEER_HEADS = 8
PEER_NKEYS = 128
PEER_TOPK = 16
N_SEL = PEER_HEADS * PEER_TOPK

SUBLANES = 8
LANES = 128
VMEM_LIMIT = 56 * 1024 * 1024

INPROJ_ROWS = 512
ATTN_QUERIES = 512
OUTPROJ_ROWS = 512
TOPK_TOKENS = 512
PEER_TOKENS = 512

LOG2_E = 1.4426950408889634
NT_DIMS = (((1,), (1,)), ((), ()))


def _dot(a, b):
    return jnp.dot(a, b, preferred_element_type=F32)


def _dot_nt(a, b):
    return lax.dot_general(a, b, NT_DIMS, preferred_element_type=F32)


def _split_bf16(x):
    hi = x.astype(BF16)
    return hi, (x - hi.astype(F32)).astype(BF16)


def _gelu_tanh(x):
    k = -2.0 * LOG2_E * (2.0 / np.pi) ** 0.5
    return x / (1.0 + jnp.exp2(x * (k + (k * 0.044715) * (x * x))))


def _rms_scale(x):
    return lax.rsqrt(jnp.mean(x * x, axis=-1, keepdims=True) + EPS)


def _mod_kernel(c_ref, w_ref, b_ref, o_ref):
    s = jax.nn.silu(c_ref[...])
    s_hi, s_lo = _split_bf16(s)
    w_hi, w_lo = _split_bf16(w_ref[...])
    o_ref[...] = _dot(s_hi, w_hi) + _dot(s_lo, w_hi) + _dot(s_hi, w_lo) + b_ref[...]


def _modulation(c_all, w_mod, b_mod):
    m, d = c_all.shape
    n = w_mod.shape[1]
    tn = 1536
    return pl.pallas_call(
        _mod_kernel,
        grid=(n // tn,),
        in_specs=[pl.BlockSpec((m, d), lambda j: (0, 0)),
                  pl.BlockSpec((d, tn), lambda j: (0, j)),
                  pl.BlockSpec((1, tn), lambda j: (0, j))],
        out_specs=pl.BlockSpec((m, tn), lambda j: (0, j)),
        out_shape=jax.ShapeDtypeStruct((m, n), F32),
        compiler_params=pltpu.CompilerParams(dimension_semantics=("arbitrary",), vmem_limit_bytes=VMEM_LIMIT),
        name="modulation",
    )(c_all, w_mod, b_mod.reshape(1, n))


def _head_rms(y, gain, bd):
    hi, lo = _split_bf16(y * y)
    ss = _dot(hi, bd) + _dot(lo, bd)
    return y * lax.rsqrt(ss * (1.0 / HEAD_DIM) + EPS) * gain


def _rope(y, cos, sin):
    n = y.shape[-1]
    lane = lax.broadcasted_iota(jnp.int32, y.shape, 1)
    first_half = (lane % (2 * ROPE_FREQS)) < ROPE_FREQS
    swapped = jnp.where(first_half, pltpu.roll(y, n - ROPE_FREQS, 1), pltpu.roll(y, ROPE_FREQS, 1))
    return y * cos + swapped * sin


def _inproj_latent_kernel(x_ref, sh_ref, sc_ref, g_ref, w_ref, cos_ref, sin_ref, qg_ref, kg_ref, bd_ref,
                          q_ref, k_ref, v_ref, l_ref, gz_ref):
    x = x_ref[0]
    h = x * _rms_scale(x) * g_ref[...] * (1.0 + sc_ref[0]) + sh_ref[0]
    p = _dot(h.astype(BF16), w_ref[...])
    cos, sin = cos_ref[...], sin_ref[...]
    bd = bd_ref[...]
    q = _head_rms(p[:, :D_ATTN], qg_ref[...], bd)
    q = _rope(q, jnp.concatenate([cos] * 4, axis=1), jnp.concatenate([sin] * 4, axis=1)) * (HEAD_DIM ** -0.5 * LOG2_E)
    _store_heads(q_ref, q, N_Q_HEADS)
    k = _head_rms(p[:, D_ATTN:D_ATTN + D_KV], kg_ref[...], bd[:D_KV, :D_KV])
    _store_heads(k_ref, _rope(k, cos, sin), N_KV_HEADS)
    _store_heads(v_ref, p[:, D_ATTN + D_KV:D_ATTN + 2 * D_KV], N_KV_HEADS)
    l_ref[0] = p[:, D_ATTN + 2 * D_KV:D_ATTN + 2 * D_KV + D_LRU]
    gz_ref[0] = jax.nn.gelu(p[:, D_ATTN + 2 * D_KV + D_LRU:])


def _store_heads(ref, y, n_heads):
    for h in range(n_heads):
        ref[0, h] = y[:, h * HEAD_DIM:(h + 1) * HEAD_DIM].astype(BF16)


def _inproj_context_kernel(x_ref, sh_ref, sc_ref, g_ref, w_ref, kg_ref, bd_ref, k_ref, v_ref, l_ref):
    x = x_ref[0]
    h = x * _rms_scale(x) * g_ref[...] * (1.0 + sc_ref[...]) + sh_ref[...]
    p = _dot(h.astype(BF16), w_ref[...])
    _store_heads(k_ref, _head_rms(p[:, :D_KV], kg_ref[...], bd_ref[...]), N_KV_HEADS)
    _store_heads(v_ref, p[:, D_KV:2 * D_KV], N_KV_HEADS)
    l_ref[0] = p[:, 2 * D_KV:]


def _full(shape):
    nd = len(shape)
    return pl.BlockSpec(shape, lambda *_: (0,) * nd)


def _inproj_latent(x, sh, sc, gpre, w_bf, cos, sin, qg, kg, bd, tm):
    B, T, D = x.shape
    tok = lambda n: pl.BlockSpec((1, tm, n), lambda b, i: (b, i, 0))
    vec = pl.BlockSpec((1, 1, D), lambda b, i: (b, 0, 0))
    rope = pl.BlockSpec((tm, LANES), lambda b, i: (i, 0))
    heads = lambda n: pl.BlockSpec((1, n, tm, HEAD_DIM), lambda b, i: (b, 0, i, 0))
    head_shape = lambda n: jax.ShapeDtypeStruct((B, n, T, HEAD_DIM), BF16)
    return pl.pallas_call(
        _inproj_latent_kernel,
        grid=(B, T // tm),
        in_specs=[tok(D), vec, vec, _full((1, D)), _full(w_bf.shape), rope, rope,
                  _full((1, D_ATTN)), _full((1, D_KV)), _full((D_ATTN, D_ATTN))],
        out_specs=[heads(N_Q_HEADS), heads(N_KV_HEADS), heads(N_KV_HEADS), tok(D_LRU), tok(D_LRU)],
        out_shape=[head_shape(N_Q_HEADS), head_shape(N_KV_HEADS), head_shape(N_KV_HEADS),
                   jax.ShapeDtypeStruct((B, T, D_LRU), F32), jax.ShapeDtypeStruct((B, T, D_LRU), F32)],
        compiler_params=pltpu.CompilerParams(dimension_semantics=("arbitrary", "arbitrary"),
                                             vmem_limit_bytes=VMEM_LIMIT),
        name="inproj_latent",
    )(x, sh, sc, gpre, w_bf, cos, sin, qg, kg, bd)


def _inproj_context(ctx, sh, sc, gpre, w_bf, kg, bd, tm):
    B, C, D = ctx.shape
    tok = lambda n: pl.BlockSpec((1, tm, n), lambda b, i: (b, i, 0))
    heads = pl.BlockSpec((1, N_KV_HEADS, tm, HEAD_DIM), lambda b, i: (b, 0, i, 0))
    head_shape = jax.ShapeDtypeStruct((B, N_KV_HEADS, C, HEAD_DIM), BF16)
    return pl.pallas_call(
        _inproj_context_kernel,
        grid=(B, C // tm),
        in_specs=[tok(D), _full((1, D)), _full((1, D)), _full((1, D)), _full(w_bf.shape),
                  _full((1, D_KV)), _full((D_KV, D_KV))],
        out_specs=[heads, heads, tok(D_LRU)],
        out_shape=[head_shape, head_shape, jax.ShapeDtypeStruct((B, C, D_LRU), F32)],
        compiler_params=pltpu.CompilerParams(dimension_semantics=("arbitrary", "arbitrary"),
                                             vmem_limit_bytes=VMEM_LIMIT),
        name="inproj_context",
    )(ctx, sh, sc, gpre, w_bf, kg, bd)


GATE_ROWS = 256
HALO = SUBLANES


def _conv_chunk(src_ref, start, n_rows, cw, cb):
    lo, hi = start - HALO, start + GATE_ROWS + HALO
    parts = []
    if lo < 0:
        parts.append(jnp.zeros((HALO, D_LRU), F32))
        lo = 0
    body_hi = min(hi, n_rows)
    parts.append(src_ref[0, lo:body_hi, :])
    if hi > n_rows:
        parts.append(jnp.zeros((HALO, D_LRU), F32))
    xh = jnp.concatenate(parts, axis=0) if len(parts) > 1 else parts[0]
    out = cb
    for j in range(CONV_W):
        off = HALO + j - CONV_PAD_LEFT
        out = out + xh[off:off + GATE_ROWS, :] * cw[j:j + 1, :]
    return out


def _tile_scan(a, b, carry, row, reverse):
    for sh in (1, 2, 4):
        if reverse:
            a_n, b_n = pltpu.roll(a, SUBLANES - sh, 0), pltpu.roll(b, SUBLANES - sh, 0)
            valid = row < SUBLANES - sh
        else:
            a_n, b_n = pltpu.roll(a, sh, 0), pltpu.roll(b, sh, 0)
            valid = row >= sh
        b = jnp.where(valid, a * b_n + b, b)
        a = jnp.where(valid, a * a_n, a)
    h = a * carry + b
    new_carry = h[0:1, :] if reverse else h[SUBLANES - 1:SUBLANES, :]
    return h, new_carry


def _rglru_kernel(lc_ref, lx_ref, gz_ref, cw_ref, cb_ref, wa_ref, wi_ref, ba_ref, bi_ref, lam_ref, o_ref,
                  a0_ref, b0_ref, a1_ref, b1_ref, *, n_ctx, n_lat):
    cw, cb = cw_ref[...], cb_ref[...]
    a_refs, b_refs = (a0_ref, a1_ref), (b0_ref, b1_ref)
    for seg_ref, seg_rows, seg_off in ((lc_ref, n_ctx, 0), (lx_ref, n_lat, n_ctx)):
        for s in range(0, seg_rows, GATE_ROWS):
            xc = _conv_chunk(seg_ref, s, seg_rows, cw, cb)
            xb = xc.astype(BF16)
            for d in range(2):
                r = jax.nn.sigmoid(_dot(xb, wa_ref[d]) + ba_ref[d])
                i = jax.nn.sigmoid(_dot(xb, wi_ref[d]) + bi_ref[d])
                log_a = -RG_C * r * jax.nn.softplus(-lam_ref[d])
                a = jnp.exp(log_a)
                b = jnp.sqrt(jnp.maximum(1.0 - a * a, 0.0)) * (i * xc)
                a_refs[d][seg_off + s:seg_off + s + GATE_ROWS, :] = a
                b_refs[d][seg_off + s:seg_off + s + GATE_ROWS, :] = b

    ctx_tiles, all_tiles = n_ctx // SUBLANES, (n_ctx + n_lat) // SUBLANES
    row = lax.broadcasted_iota(jnp.int32, (SUBLANES, D_LRU), 0)

    def step(j, carry):
        cf, cr = carry
        tf = pl.multiple_of(j * SUBLANES, SUBLANES)
        jr = jnp.where(j < ctx_tiles, ctx_tiles - 1 - j, all_tiles + ctx_tiles - 1 - j)
        tr = pl.multiple_of(jr * SUBLANES, SUBLANES)
        hf, cf = _tile_scan(a0_ref[pl.ds(tf, SUBLANES), :], b0_ref[pl.ds(tf, SUBLANES), :], cf, row, False)
        hr, cr = _tile_scan(a1_ref[pl.ds(tr, SUBLANES), :], b1_ref[pl.ds(tr, SUBLANES), :], cr, row, True)
        b0_ref[pl.ds(tf, SUBLANES), :] = hf
        b1_ref[pl.ds(tr, SUBLANES), :] = hr
        return cf, cr

    zero = jnp.zeros((1, D_LRU), F32)
    lax.fori_loop(0, all_tiles, step, (zero, zero))
    for s in range(0, n_lat, GATE_ROWS):
        y = b0_ref[n_ctx + s:n_ctx + s + GATE_ROWS, :] + b1_ref[n_ctx + s:n_ctx + s + GATE_ROWS, :]
        o_ref[0, s:s + GATE_ROWS, :] = (y * gz_ref[0, s:s + GATE_ROWS, :]).astype(BF16)


def _rglru(lc, lx, gz, conv_w, conv_b, wa_bd, wi_bd, ba, bi, lam):
    B, C, _ = lc.shape
    T = lx.shape[1]
    seq = lambda n: pl.BlockSpec((1, n, D_LRU), lambda b: (b, 0, 0))
    scratch = [pltpu.VMEM((C + T, D_LRU), F32) for _ in range(4)]
    return pl.pallas_call(
        functools.partial(_rglru_kernel, n_ctx=C, n_lat=T),
        grid=(B,),
        in_specs=[seq(C), seq(T), seq(T), _full((CONV_W, D_LRU)), _full((1, D_LRU)),
                  _full((2, D_LRU, D_LRU)), _full((2, D_LRU, D_LRU)),
                  _full((2, 1, D_LRU)), _full((2, 1, D_LRU)), _full((2, 1, D_LRU))],
        out_specs=seq(T),
        out_shape=jax.ShapeDtypeStruct((B, T, D_LRU), BF16),
        scratch_shapes=scratch,
        compiler_params=pltpu.CompilerParams(dimension_semantics=("arbitrary",), vmem_limit_bytes=VMEM_LIMIT),
        name="rglru",
    )(lc, lx, gz, conv_w, conv_b, wa_bd, wi_bd, ba, bi, lam)


def _attn_kernel(q_ref, k_ref, v_ref, o_ref, *, tq):
    group = N_Q_HEADS // N_KV_HEADS
    for pair in range(N_Q_HEADS // 2):
        g = 2 * pair // group
        q = q_ref[0, 2 * pair:2 * pair + 2].reshape(2 * tq, HEAD_DIM)
        s = _dot_nt(q, k_ref[0, g])
        p = jnp.exp2(s - jnp.max(s, axis=-1, keepdims=True))
        denom = jnp.sum(p, axis=-1, keepdims=True)
        o = _dot(p.astype(BF16), v_ref[0, g]) / denom
        o_ref[0, :, 2 * pair * HEAD_DIM:(2 * pair + 2) * HEAD_DIM] = (
            jnp.concatenate([o[:tq], o[tq:]], axis=1).astype(BF16))


def _attention(q4, k4, v4, tq):
    B, H, T, _ = q4.shape
    S = k4.shape[2]
    kv = pl.BlockSpec((1, N_KV_HEADS, S, HEAD_DIM), lambda b, i: (b, 0, 0, 0))
    return pl.pallas_call(
        functools.partial(_attn_kernel, tq=tq),
        grid=(B, T // tq),
        in_specs=[pl.BlockSpec((1, H, tq, HEAD_DIM), lambda b, i: (b, 0, i, 0)), kv, kv],
        out_specs=pl.BlockSpec((1, tq, H * HEAD_DIM), lambda b, i: (b, i, 0)),
        out_shape=jax.ShapeDtypeStruct((B, T, H * HEAD_DIM), BF16),
        compiler_params=pltpu.CompilerParams(dimension_semantics=("arbitrary", "arbitrary"),
                                             vmem_limit_bytes=VMEM_LIMIT),
        name="attention",
    )(q4, k4, v4)


def _outproj_kernel(attn_ref, lru_ref, x_ref, wout_ref, gpm_ref, g1_ref, gpf_ref, sh_ref, sc_ref, wq_ref,
                    xmid_ref, hx_ref, qp_ref):
    mix = _dot(attn_ref[0], wout_ref[:D_ATTN, :]) + _dot(lru_ref[0], wout_ref[D_ATTN:, :])
    x_mid = x_ref[0] + g1_ref[0] * (mix * _rms_scale(mix) * gpm_ref[...])
    xmid_ref[0] = x_mid
    hx = (x_mid * _rms_scale(x_mid) * gpf_ref[...] * (1.0 + sc_ref[0]) + sh_ref[0]).astype(BF16)
    hx_ref[0] = hx
    qp_ref[0] = _dot(hx, wq_ref[...]).astype(BF16)


def _outproj(attn, lru, x, wout_bf, gpm, g1, gpf, sh2, sc2, wq_bf, tm):
    B, T, D = x.shape
    nq = wq_bf.shape[1]
    tok = lambda n: pl.BlockSpec((1, tm, n), lambda b, i: (b, i, 0))
    vec = pl.BlockSpec((1, 1, D), lambda b, i: (b, 0, 0))
    return pl.pallas_call(
        _outproj_kernel,
        grid=(B, T // tm),
        in_specs=[tok(D_ATTN), tok(D_LRU), tok(D), _full(wout_bf.shape), _full((1, D)), vec, _full((1, D)),
                  vec, vec, _full(wq_bf.shape)],
        out_specs=[tok(D), tok(D), tok(nq)],
        out_shape=[jax.ShapeDtypeStruct((B, T, D), F32), jax.ShapeDtypeStruct((B, T, D), BF16),
                   jax.ShapeDtypeStruct((B, T, nq), BF16)],
        compiler_params=pltpu.CompilerParams(dimension_semantics=("arbitrary", "arbitrary"),
                                             vmem_limit_bytes=VMEM_LIMIT),
        name="outproj",
    )(attn, lru, x, wout_bf, gpm, g1, gpf, sh2, sc2, wq_bf)


NEG_INF = float("-inf")
BIG = 1e9
SORT_GROUP = 4


def _slab_argmax(vs, cs):
    while len(vs) > 1:
        nv, nc = [], []
        for i in range(0, len(vs) - 1, 2):
            take = vs[i + 1] > vs[i]
            nv.append(jnp.where(take, vs[i + 1], vs[i]))
            nc.append(jnp.where(take, cs[i + 1], cs[i]))
        if len(vs) % 2:
            nv.append(vs[-1])
            nc.append(cs[-1])
        vs, cs = nv, nc
    return vs[0], cs[0]


def _extract_topk(vs, cs, k):
    n = vs[0].shape[1]
    slot = lax.broadcasted_iota(jnp.int32, (k, n), 0)
    top_v = jnp.zeros((k, n), F32)
    top_c = jnp.zeros((k, n), F32)
    groups = []
    for i in range(0, len(vs), SORT_GROUP):
        gv, gc = list(vs[i:i + SORT_GROUP]), list(cs[i:i + SORT_GROUP])
        for end in range(len(gv) - 1, 0, -1):
            for a in range(end):
                take = gv[a + 1] > gv[a]
                gv[a], gv[a + 1] = jnp.where(take, gv[a + 1], gv[a]), jnp.where(take, gv[a], gv[a + 1])
                gc[a], gc[a + 1] = jnp.where(take, gc[a + 1], gc[a]), jnp.where(take, gc[a], gc[a + 1])
        groups.append((gv, gc))
    for j in range(k):
        v, c = _slab_argmax([gv[0] for gv, _ in groups], [gc[0] for _, gc in groups])
        m = jnp.max(v, axis=0, keepdims=True)
        cm = jnp.min(jnp.where(v == m, c, BIG), axis=0, keepdims=True)
        top_v = jnp.where(slot == j, m, top_v)
        top_c = jnp.where(slot == j, cm, top_c)
        for gv, gc in groups:
            hit = gc[0] == cm
            for d in range(len(gv) - 1):
                gv[d] = jnp.where(hit, gv[d + 1], gv[d])
                gc[d] = jnp.where(hit, gc[d + 1], gc[d])
            gv[-1] = jnp.where(hit, NEG_INF, gv[-1])
    return top_v, top_c


def _lookup(table, pos):
    out = jnp.zeros(pos.shape, table.dtype)
    for i in range(PEER_TOPK):
        out = jnp.where(pos == i, table[i:i + 1, :], out)
    return out


def _topk_kernel(qp_ref, keys_ref, ia_ref, ib_ref, g_ref, *, tn):
    half = SUBLANES
    sub = lax.broadcasted_iota(jnp.int32, (half, tn), 0).astype(F32)
    key_codes = [sub + float(half * s) for s in range(PEER_NKEYS // half)]
    cand_codes = [sub, sub + float(half)] + [sub + 16.0 * i for i in range(1, half)] + [(sub + float(half)) * 16.0]
    for h in range(PEER_HEADS):
        tops = []
        for p in range(2):
            col = (2 * h + p) * PEER_NKEYS
            s = _dot_nt(keys_ref[2 * h + p], qp_ref[:, col:col + PEER_NKEYS])
            tops.append(_extract_topk([s[half * r:half * (r + 1), :] for r in range(PEER_NKEYS // half)],
                                      key_codes, PEER_TOPK))
        (v1, i1), (v2, i2) = tops
        groups = [v1[0:1, :] + v2[:half, :], v1[0:1, :] + v2[half:, :]]
        groups += [v1[i:i + 1, :] + v2[:half, :] for i in range(1, half)]
        groups += [v1[half:, :] + v2[0:1, :]]
        top_s, top_c = _extract_topk(groups, cand_codes, PEER_TOPK)
        e = jnp.exp(top_s - top_s[0:1, :])
        rows = slice(h * PEER_TOPK, (h + 1) * PEER_TOPK)
        g_ref[rows, :] = e / jnp.sum(e, axis=0, keepdims=True)
        pair = top_c.astype(jnp.int32)
        ia_ref[rows, :] = _lookup(i1, pair >> 4).astype(jnp.int32)
        ib_ref[rows, :] = _lookup(i2, pair & 15).astype(jnp.int32)


def _peer_topk(qp, keys_bf, tn):
    n, nq = qp.shape
    out = pl.BlockSpec((N_SEL, tn), lambda i: (0, i))
    return pl.pallas_call(
        functools.partial(_topk_kernel, tn=tn),
        grid=(n // tn,),
        in_specs=[pl.BlockSpec((tn, nq), lambda i: (i, 0)), _full(keys_bf.shape)],
        out_specs=[out, out, out],
        out_shape=[jax.ShapeDtypeStruct((N_SEL, n), jnp.int32), jax.ShapeDtypeStruct((N_SEL, n), jnp.int32),
                   jax.ShapeDtypeStruct((N_SEL, n), F32)],
        compiler_params=pltpu.CompilerParams(dimension_semantics=("arbitrary",), vmem_limit_bytes=VMEM_LIMIT),
        name="peer_topk",
    )(qp, keys_bf)


EXPERT_CHUNK = 2048
PEER_VMEM_LIMIT = 62 * 1024 * 1024
GROUPS_PER_CHUNK = EXPERT_CHUNK // PEER_NKEYS
TOKEN_UNROLL = 64
GROUP_PAD = 4


def _peer_kernel(hx_ref, ia_ref, ib_ref, g_ref, ut_ref, v_ref, xmid_ref, g2_ref, gpost_ref, o_ref,
                 s_ref, *, tb, n_chunks):
    j = pl.program_id(1)
    srow = tb + GROUP_PAD

    def group_rows(chunk, l):
        return pl.ds(pl.multiple_of(chunk * (GROUPS_PER_CHUNK * srow), SUBLANES) + l * srow, tb)

    @pl.when(j < n_chunks)
    def _scores():
        act = _gelu_tanh(_dot(hx_ref[...], ut_ref[0]))
        for l in range(GROUPS_PER_CHUNK):
            s_ref[group_rows(j, l), :] = act[:, l * PEER_NKEYS:(l + 1) * PEER_NKEYS]

    @pl.when(j == n_chunks)
    def _select():
        sub = lax.broadcasted_iota(jnp.int32, (PEER_NKEYS, N_SEL), 0)

        def tokens(tt, carry):
            for u in range(TOKEN_UNROLL):
                t = tt * TOKEN_UNROLL + u
                first = jnp.where(sub == ia_ref[t], g_ref[t], 0.0).astype(BF16)
                second = jnp.where(sub == ib_ref[t], 1.0, 0.0).astype(BF16)
                rows = pl.ds(t, PEER_NKEYS, stride=srow)
                s_ref[rows, :] = _dot_nt(first, second) * s_ref[rows, :]
            return carry

        lax.fori_loop(0, tb // TOKEN_UNROLL, tokens, 0)

    @pl.when(j >= n_chunks)
    def _combine():
        jj = j - n_chunks
        h = jnp.concatenate([s_ref[group_rows(jj, l), :] for l in range(GROUPS_PER_CHUNK)], axis=1)
        part = _dot(h.astype(BF16), v_ref[...])

        @pl.when(jj == 0)
        def _():
            o_ref[...] = part

        @pl.when(jj > 0)
        def _():
            o_ref[...] += part

    @pl.when(j == 2 * n_chunks - 1)
    def _finish():
        f = o_ref[...]
        o_ref[...] = xmid_ref[...] + g2_ref[0] * (f * _rms_scale(f) * gpost_ref[...])


def _peer_dense(hx, ia3, ib3, g3, ut_bf, v_bf, xmid, g2, gpost, tb, tokens_per_batch):
    n, d = hx.shape
    n_chunks = ut_bf.shape[0]
    bpb = tokens_per_batch // tb
    tok = pl.BlockSpec((tb, d), lambda i, j: (i, 0))
    tok_once = pl.BlockSpec((tb, d), lambda i, j: (i, 0), pipeline_mode=pl.Buffered(1))
    pick = pl.BlockSpec((tb, 1, N_SEL), lambda i, j: (i, 0, 0))
    return pl.pallas_call(
        functools.partial(_peer_kernel, tb=tb, n_chunks=n_chunks),
        grid=(n // tb, 2 * n_chunks),
        in_specs=[tok_once, pick, pick, pick,
                  pl.BlockSpec((1, d, EXPERT_CHUNK), lambda i, j: (jnp.minimum(j, n_chunks - 1), 0, 0)),
                  pl.BlockSpec((EXPERT_CHUNK, d), lambda i, j: (jnp.maximum(j - n_chunks, 0), 0)),
                  tok_once, pl.BlockSpec((1, 1, d), lambda i, j: (i // bpb, 0, 0)), _full((1, d))],
        out_specs=tok,
        out_shape=jax.ShapeDtypeStruct((n, d), F32),
        scratch_shapes=[pltpu.VMEM((PEER_NKEYS * (tb + GROUP_PAD), PEER_NKEYS), F32)],
        compiler_params=pltpu.CompilerParams(dimension_semantics=("arbitrary", "arbitrary"),
                                             vmem_limit_bytes=PEER_VMEM_LIMIT),
        name="peer_dense",
    )(hx, ia3, ib3, g3, ut_bf, v_bf, xmid, g2, gpost)


def _rope_tables(T):
    t = np.arange(T)
    inv = ROPE_THETA ** (-jnp.arange(ROPE_FREQS, dtype=F32) / ROPE_FREQS)
    ang_r = jnp.asarray(t // GRID_W, F32)[:, None] * inv
    ang_c = jnp.asarray(t % GRID_W, F32)[:, None] * inv
    cos = jnp.concatenate([jnp.cos(ang_r)] * 2 + [jnp.cos(ang_c)] * 2, axis=1)
    sin = jnp.concatenate([-jnp.sin(ang_r), jnp.sin(ang_r), -jnp.sin(ang_c), jnp.sin(ang_c)], axis=1)
    return jnp.concatenate([cos, cos], axis=1), jnp.concatenate([sin, sin], axis=1)


def _block_diag(w):
    n, m, _ = w.shape
    eye = jnp.eye(n, dtype=w.dtype)
    return (eye[:, None, :, None] * w[:, :, None, :]).reshape(n * m, n * m)


def kernel(x, c, ctx, c_ctx, w_mod, b_mod, g_pre_mix, g_post_mix, g_pre_ffn, g_post_ffn, w_in, q_norm_g, k_norm_g, conv_w, conv_b, rg_wa, rg_ba, rg_wi, rg_bi, rg_lambda, w_out, peer_wq, peer_subkeys, peer_u, peer_v):
    B, T, D = x.shape
    C = ctx.shape[1]
    depth = w_mod.shape[0]
    cos, sin = _rope_tables(T)
    head_ones = _block_diag(jnp.ones((N_Q_HEADS, HEAD_DIM, HEAD_DIM), BF16))
    mod_rows = -(-(B + 1) // SUBLANES) * SUBLANES

    for l in range(depth):
        assert l == depth - 1, "context-stream update between layers is not implemented"
        c_all = jnp.zeros((mod_rows, D), F32).at[:B].set(c).at[B].set(c_ctx)
        mod = _modulation(c_all, w_mod[l], b_mod[l])
        mx = mod[:B].reshape(B, N_MOD, 1, D)
        sh1, sc1, g1, sh2, sc2, g2 = (mx[:, i] for i in range(N_MOD))
        mc = mod[B].reshape(N_MOD, 1, D)

        w_bf = w_in[l].astype(BF16)
        qg = jnp.tile(q_norm_g[l], N_Q_HEADS).reshape(1, D_ATTN)
        kg = jnp.tile(k_norm_g[l], N_KV_HEADS).reshape(1, D_KV)
        gpre = g_pre_mix[l].reshape(1, D)
        q, kx, vx, lx, gz = _inproj_latent(x, sh1, sc1, gpre, w_bf, cos, sin, qg, kg, head_ones,
                                           tm=min(INPROJ_ROWS, T))
        kc, vc, lc = _inproj_context(ctx, mc[0], mc[1], gpre, w_bf[:, D_ATTN:D_ATTN + 2 * D_KV + D_LRU],
                                     kg, head_ones[:D_KV, :D_KV], tm=C)

        lru = _rglru(lc, lx, gz, conv_w[l], conv_b[l].reshape(1, D_LRU),
                     jnp.stack([_block_diag(rg_wa[l, d]) for d in range(2)]).astype(BF16),
                     jnp.stack([_block_diag(rg_wi[l, d]) for d in range(2)]).astype(BF16),
                     rg_ba[l].reshape(2, 1, D_LRU), rg_bi[l].reshape(2, 1, D_LRU), rg_lambda[l].reshape(2, 1, D_LRU))

        attn = _attention(q, jnp.concatenate([kc, kx], axis=2), jnp.concatenate([vc, vx], axis=2),
                          tq=min(ATTN_QUERIES, T))

        x_mid, hx, qp = _outproj(attn, lru, x, w_out[l].astype(BF16), g_post_mix[l].reshape(1, D), g1,
                                 g_pre_ffn[l].reshape(1, D), sh2, sc2, peer_wq[l].astype(BF16),
                                 tm=min(OUTPROJ_ROWS, T))

        n = B * T
        keys_bf = peer_subkeys[l].reshape(2 * PEER_HEADS, PEER_NKEYS, -1).astype(BF16)
        ia, ib, gate = _peer_topk(qp.reshape(n, -1), keys_bf, tn=TOPK_TOKENS)
        picks = lambda a: a.T.reshape(n, 1, N_SEL)
        ut = peer_u[l].astype(BF16).reshape(-1, EXPERT_CHUNK, D).transpose(0, 2, 1)
        x = _peer_dense(hx.reshape(n, D), picks(ia), picks(ib), picks(gate), ut,
                        peer_v[l].astype(BF16), x_mid.reshape(n, D), g2, g_post_ffn[l].reshape(1, D),
                        tb=min(PEER_TOKENS, T), tokens_per_batch=T).reshape(B, T, D)
    return x
```

```python
import functools

import jax
import jax.numpy as jnp
import numpy as np
from jax import lax
from jax.experimental import pallas as pl
from jax.experimental.pallas import tpu as pltpu

F32 = jnp.float32
BF16 = jnp.bfloat16

HEAD_DIM = 64
N_Q_HEADS = 8
N_KV_HEADS = 2
D_ATTN = 512
D_KV = 128
D_LRU = 512
LRU_BLOCKS = 8
CONV_W = 4
CONV_PAD_LEFT = 2
RG_C = 8.0
ROPE_THETA = 10000.0
ROPE_FREQS = 16
GRID_W = 64
N_MOD = 6
EPS = 1e-6
PEER_HEADS = 8
PEER_NKEYS = 128
PEER_TOPK = 16
N_SEL = PEER_HEADS * PEER_TOPK

SUBLANES = 8
LANES = 128
VMEM_LIMIT = 56 * 1024 * 1024

INPROJ_ROWS = 512
ATTN_QUERIES = 512
OUTPROJ_ROWS = 512
TOPK_TOKENS = 1024
PEER_TOKENS = 512

LOG2_E = 1.4426950408889634
NT_DIMS = (((1,), (1,)), ((), ()))


def _dot(a, b):
    return jnp.dot(a, b, preferred_element_type=F32)


def _dot_nt(a, b):
    return lax.dot_general(a, b, NT_DIMS, preferred_element_type=F32)


def _split_bf16(x):
    hi = x.astype(BF16)
    return hi, (x - hi.astype(F32)).astype(BF16)


def _gelu_tanh(x):
    k = -2.0 * LOG2_E * (2.0 / np.pi) ** 0.5
    return x / (1.0 + jnp.exp2(x * (k + (k * 0.044715) * (x * x))))


def _rms_scale(x):
    return lax.rsqrt(jnp.mean(x * x, axis=-1, keepdims=True) + EPS)


def _mod_kernel(c_ref, w_ref, b_ref, o_ref):
    s = jax.nn.silu(c_ref[...])
    s_hi, s_lo = _split_bf16(s)
    w_hi, w_lo = _split_bf16(w_ref[...])
    o_ref[...] = _dot(s_hi, w_hi) + _dot(s_lo, w_hi) + _dot(s_hi, w_lo) + b_ref[...]


def _modulation(c_all, w_mod, b_mod):
    m, d = c_all.shape
    n = w_mod.shape[1]
    tn = 1536
    return pl.pallas_call(
        _mod_kernel,
        grid=(n // tn,),
        in_specs=[pl.BlockSpec((m, d), lambda j: (0, 0)),
                  pl.BlockSpec((d, tn), lambda j: (0, j)),
                  pl.BlockSpec((1, tn), lambda j: (0, j))],
        out_specs=pl.BlockSpec((m, tn), lambda j: (0, j)),
        out_shape=jax.ShapeDtypeStruct((m, n), F32),
        compiler_params=pltpu.CompilerParams(dimension_semantics=("arbitrary",), vmem_limit_bytes=VMEM_LIMIT),
        name="modulation",
    )(c_all, w_mod, b_mod.reshape(1, n))


def _head_rms(y, gain, bd):
    hi, lo = _split_bf16(y * y)
    ss = _dot(hi, bd) + _dot(lo, bd)
    return y * lax.rsqrt(ss * (1.0 / HEAD_DIM) + EPS) * gain


def _rope(y, cos, sin):
    n = y.shape[-1]
    lane = lax.broadcasted_iota(jnp.int32, y.shape, 1)
    first_half = (lane % (2 * ROPE_FREQS)) < ROPE_FREQS
    swapped = jnp.where(first_half, pltpu.roll(y, n - ROPE_FREQS, 1), pltpu.roll(y, ROPE_FREQS, 1))
    return y * cos + swapped * sin


def _inproj_latent_kernel(x_ref, sh_ref, sc_ref, g_ref, w_ref, cos_ref, sin_ref, qg_ref, kg_ref, bd_ref,
                          q_ref, k_ref, v_ref, l_ref, gz_ref):
    x = x_ref[0]
    h = x * _rms_scale(x) * g_ref[...] * (1.0 + sc_ref[0]) + sh_ref[0]
    p = _dot(h.astype(BF16), w_ref[...])
    cos, sin = cos_ref[...], sin_ref[...]
    bd = bd_ref[...]
    q = _head_rms(p[:, :D_ATTN], qg_ref[...], bd)
    q = _rope(q, jnp.concatenate([cos] * 4, axis=1), jnp.concatenate([sin] * 4, axis=1)) * (HEAD_DIM ** -0.5 * LOG2_E)
    _store_heads(q_ref, q, N_Q_HEADS)
    k = _head_rms(p[:, D_ATTN:D_ATTN + D_KV], kg_ref[...], bd[:D_KV, :D_KV])
    _store_heads(k_ref, _rope(k, cos, sin), N_KV_HEADS)
    _store_heads(v_ref, p[:, D_ATTN + D_KV:D_ATTN + 2 * D_KV], N_KV_HEADS)
    l_ref[0] = p[:, D_ATTN + 2 * D_KV:D_ATTN + 2 * D_KV + D_LRU]
    gz_ref[0] = jax.nn.gelu(p[:, D_ATTN + 2 * D_KV + D_LRU:])


def _store_heads(ref, y, n_heads):
    for h in range(n_heads):
        ref[0, h] = y[:, h * HEAD_DIM:(h + 1) * HEAD_DIM].astype(BF16)


def _inproj_context_kernel(x_ref, sh_ref, sc_ref, g_ref, w_ref, kg_ref, bd_ref, k_ref, v_ref, l_ref):
    x = x_ref[0]
    h = x * _rms_scale(x) * g_ref[...] * (1.0 + sc_ref[...]) + sh_ref[...]
    p = _dot(h.astype(BF16), w_ref[...])
    _store_heads(k_ref, _head_rms(p[:, :D_KV], kg_ref[...], bd_ref[...]), N_KV_HEADS)
    _store_heads(v_ref, p[:, D_KV:2 * D_KV], N_KV_HEADS)
    l_ref[0] = p[:, 2 * D_KV:]


def _full(shape):
    nd = len(shape)
    return pl.BlockSpec(shape, lambda *_: (0,) * nd)


def _inproj_latent(x, sh, sc, gpre, w_bf, cos, sin, qg, kg, bd, tm):
    B, T, D = x.shape
    tok = lambda n: pl.BlockSpec((1, tm, n), lambda b, i: (b, i, 0))
    vec = pl.BlockSpec((1, 1, D), lambda b, i: (b, 0, 0))
    rope = pl.BlockSpec((tm, LANES), lambda b, i: (i, 0))
    heads = lambda n: pl.BlockSpec((1, n, tm, HEAD_DIM), lambda b, i: (b, 0, i, 0))
    head_shape = lambda n: jax.ShapeDtypeStruct((B, n, T, HEAD_DIM), BF16)
    return pl.pallas_call(
        _inproj_latent_kernel,
        grid=(B, T // tm),
        in_specs=[tok(D), vec, vec, _full((1, D)), _full(w_bf.shape), rope, rope,
                  _full((1, D_ATTN)), _full((1, D_KV)), _full((D_ATTN, D_ATTN))],
        out_specs=[heads(N_Q_HEADS), heads(N_KV_HEADS), heads(N_KV_HEADS), tok(D_LRU), tok(D_LRU)],
        out_shape=[head_shape(N_Q_HEADS), head_shape(N_KV_HEADS), head_shape(N_KV_HEADS),
                   jax.ShapeDtypeStruct((B, T, D_LRU), F32), jax.ShapeDtypeStruct((B, T, D_LRU), F32)],
        compiler_params=pltpu.CompilerParams(dimension_semantics=("arbitrary", "arbitrary"),
                                             vmem_limit_bytes=VMEM_LIMIT),
        name="inproj_latent",
    )(x, sh, sc, gpre, w_bf, cos, sin, qg, kg, bd)


def _inproj_context(ctx, sh, sc, gpre, w_bf, kg, bd, tm):
    B, C, D = ctx.shape
    tok = lambda n: pl.BlockSpec((1, tm, n), lambda b, i: (b, i, 0))
    heads = pl.BlockSpec((1, N_KV_HEADS, tm, HEAD_DIM), lambda b, i: (b, 0, i, 0))
    head_shape = jax.ShapeDtypeStruct((B, N_KV_HEADS, C, HEAD_DIM), BF16)
    return pl.pallas_call(
        _inproj_context_kernel,
        grid=(B, C // tm),
        in_specs=[tok(D), _full((1, D)), _full((1, D)), _full((1, D)), _full(w_bf.shape),
                  _full((1, D_KV)), _full((D_KV, D_KV))],
        out_specs=[heads, heads, tok(D_LRU)],
        out_shape=[head_shape, head_shape, jax.ShapeDtypeStruct((B, C, D_LRU), F32)],
        compiler_params=pltpu.CompilerParams(dimension_semantics=("arbitrary", "arbitrary"),
                                             vmem_limit_bytes=VMEM_LIMIT),
        name="inproj_context",
    )(ctx, sh, sc, gpre, w_bf, kg, bd)


GATE_ROWS = 256
HALO = SUBLANES


def _conv_chunk(src_ref, start, n_rows, cw, cb):
    lo, hi = start - HALO, start + GATE_ROWS + HALO
    parts = []
    if lo < 0:
        parts.append(jnp.zeros((HALO, D_LRU), F32))
        lo = 0
    body_hi = min(hi, n_rows)
    parts.append(src_ref[0, lo:body_hi, :])
    if hi > n_rows:
        parts.append(jnp.zeros((HALO, D_LRU), F32))
    xh = jnp.concatenate(parts, axis=0) if len(parts) > 1 else parts[0]
    out = cb
    for j in range(CONV_W):
        off = HALO + j - CONV_PAD_LEFT
        out = out + xh[off:off + GATE_ROWS, :] * cw[j:j + 1, :]
    return out


def _tile_scan(a, b, carry, row, reverse):
    for sh in (1, 2, 4):
        if reverse:
            a_n, b_n = pltpu.roll(a, SUBLANES - sh, 0), pltpu.roll(b, SUBLANES - sh, 0)
            valid = row < SUBLANES - sh
        else:
            a_n, b_n = pltpu.roll(a, sh, 0), pltpu.roll(b, sh, 0)
            valid = row >= sh
        b = jnp.where(valid, a * b_n + b, b)
        a = jnp.where(valid, a * a_n, a)
    h = a * carry + b
    new_carry = h[0:1, :] if reverse else h[SUBLANES - 1:SUBLANES, :]
    return h, new_carry


def _rglru_kernel(lc_ref, lx_ref, gz_ref, cw_ref, cb_ref, wa_ref, wi_ref, ba_ref, bi_ref, lam_ref, o_ref,
                  a0_ref, b0_ref, a1_ref, b1_ref, *, n_ctx, n_lat):
    cw, cb = cw_ref[...], cb_ref[...]
    a_refs, b_refs = (a0_ref, a1_ref), (b0_ref, b1_ref)
    for seg_ref, seg_rows, seg_off in ((lc_ref, n_ctx, 0), (lx_ref, n_lat, n_ctx)):
        for s in range(0, seg_rows, GATE_ROWS):
            xc = _conv_chunk(seg_ref, s, seg_rows, cw, cb)
            xb = xc.astype(BF16)
            for d in range(2):
                r = jax.nn.sigmoid(_dot(xb, wa_ref[d]) + ba_ref[d])
                i = jax.nn.sigmoid(_dot(xb, wi_ref[d]) + bi_ref[d])
                log_a = -RG_C * r * jax.nn.softplus(-lam_ref[d])
                a = jnp.exp(log_a)
                b = jnp.sqrt(jnp.maximum(1.0 - a * a, 0.0)) * (i * xc)
                a_refs[d][seg_off + s:seg_off + s + GATE_ROWS, :] = a
                b_refs[d][seg_off + s:seg_off + s + GATE_ROWS, :] = b

    ctx_tiles, all_tiles = n_ctx // SUBLANES, (n_ctx + n_lat) // SUBLANES
    row = lax.broadcasted_iota(jnp.int32, (SUBLANES, D_LRU), 0)

    def step(j, carry):
        cf, cr = carry
        tf = pl.multiple_of(j * SUBLANES, SUBLANES)
        jr = jnp.where(j < ctx_tiles, ctx_tiles - 1 - j, all_tiles + ctx_tiles - 1 - j)
        tr = pl.multiple_of(jr * SUBLANES, SUBLANES)
        hf, cf = _tile_scan(a0_ref[pl.ds(tf, SUBLANES), :], b0_ref[pl.ds(tf, SUBLANES), :], cf, row, False)
        hr, cr = _tile_scan(a1_ref[pl.ds(tr, SUBLANES), :], b1_ref[pl.ds(tr, SUBLANES), :], cr, row, True)
        b0_ref[pl.ds(tf, SUBLANES), :] = hf
        b1_ref[pl.ds(tr, SUBLANES), :] = hr
        return cf, cr

    zero = jnp.zeros((1, D_LRU), F32)
    lax.fori_loop(0, all_tiles, step, (zero, zero))
    for s in range(0, n_lat, GATE_ROWS):
        y = b0_ref[n_ctx + s:n_ctx + s + GATE_ROWS, :] + b1_ref[n_ctx + s:n_ctx + s + GATE_ROWS, :]
        o_ref[0, s:s + GATE_ROWS, :] = (y * gz_ref[0, s:s + GATE_ROWS, :]).astype(BF16)


def _rglru(lc, lx, gz, conv_w, conv_b, wa_bd, wi_bd, ba, bi, lam):
    B, C, _ = lc.shape
    T = lx.shape[1]
    seq = lambda n: pl.BlockSpec((1, n, D_LRU), lambda b: (b, 0, 0))
    scratch = [pltpu.VMEM((C + T, D_LRU), F32) for _ in range(4)]
    return pl.pallas_call(
        functools.partial(_rglru_kernel, n_ctx=C, n_lat=T),
        grid=(B,),
        in_specs=[seq(C), seq(T), seq(T), _full((CONV_W, D_LRU)), _full((1, D_LRU)),
                  _full((2, D_LRU, D_LRU)), _full((2, D_LRU, D_LRU)),
                  _full((2, 1, D_LRU)), _full((2, 1, D_LRU)), _full((2, 1, D_LRU))],
        out_specs=seq(T),
        out_shape=jax.ShapeDtypeStruct((B, T, D_LRU), BF16),
        scratch_shapes=scratch,
        compiler_params=pltpu.CompilerParams(dimension_semantics=("arbitrary",), vmem_limit_bytes=VMEM_LIMIT),
        name="rglru",
    )(lc, lx, gz, conv_w, conv_b, wa_bd, wi_bd, ba, bi, lam)


def _attn_kernel(q_ref, k_ref, v_ref, o_ref, *, tq):
    group = N_Q_HEADS // N_KV_HEADS
    for pair in range(N_Q_HEADS // 2):
        g = 2 * pair // group
        q = q_ref[0, 2 * pair:2 * pair + 2].reshape(2 * tq, HEAD_DIM)
        s = _dot_nt(q, k_ref[0, g])
        p = jnp.exp2(s - jnp.max(s, axis=-1, keepdims=True))
        denom = jnp.sum(p, axis=-1, keepdims=True)
        o = _dot(p.astype(BF16), v_ref[0, g]) / denom
        o_ref[0, :, 2 * pair * HEAD_DIM:(2 * pair + 2) * HEAD_DIM] = (
            jnp.concatenate([o[:tq], o[tq:]], axis=1).astype(BF16))


def _attention(q4, k4, v4, tq):
    B, H, T, _ = q4.shape
    S = k4.shape[2]
    kv = pl.BlockSpec((1, N_KV_HEADS, S, HEAD_DIM), lambda b, i: (b, 0, 0, 0))
    return pl.pallas_call(
        functools.partial(_attn_kernel, tq=tq),
        grid=(B, T // tq),
        in_specs=[pl.BlockSpec((1, H, tq, HEAD_DIM), lambda b, i: (b, 0, i, 0)), kv, kv],
        out_specs=pl.BlockSpec((1, tq, H * HEAD_DIM), lambda b, i: (b, i, 0)),
        out_shape=jax.ShapeDtypeStruct((B, T, H * HEAD_DIM), BF16),
        compiler_params=pltpu.CompilerParams(dimension_semantics=("arbitrary", "arbitrary"),
                                             vmem_limit_bytes=VMEM_LIMIT),
        name="attention",
    )(q4, k4, v4)


def _outproj_kernel(attn_ref, lru_ref, x_ref, wout_ref, gpm_ref, g1_ref, gpf_ref, sh_ref, sc_ref, wq_ref,
                    xmid_ref, hx_ref, qp_ref):
    mix = _dot(attn_ref[0], wout_ref[:D_ATTN, :]) + _dot(lru_ref[0], wout_ref[D_ATTN:, :])
    x_mid = x_ref[0] + g1_ref[0] * (mix * _rms_scale(mix) * gpm_ref[...])
    xmid_ref[0] = x_mid
    hx = (x_mid * _rms_scale(x_mid) * gpf_ref[...] * (1.0 + sc_ref[0]) + sh_ref[0]).astype(BF16)
    hx_ref[0] = hx
    qp_ref[0] = _dot(hx, wq_ref[...]).astype(BF16)


def _outproj(attn, lru, x, wout_bf, gpm, g1, gpf, sh2, sc2, wq_bf, tm):
    B, T, D = x.shape
    nq = wq_bf.shape[1]
    tok = lambda n: pl.BlockSpec((1, tm, n), lambda b, i: (b, i, 0))
    vec = pl.BlockSpec((1, 1, D), lambda b, i: (b, 0, 0))
    return pl.pallas_call(
        _outproj_kernel,
        grid=(B, T // tm),
        in_specs=[tok(D_ATTN), tok(D_LRU), tok(D), _full(wout_bf.shape), _full((1, D)), vec, _full((1, D)),
                  vec, vec, _full(wq_bf.shape)],
        out_specs=[tok(D), tok(D), tok(nq)],
        out_shape=[jax.ShapeDtypeStruct((B, T, D), F32), jax.ShapeDtypeStruct((B, T, D), BF16),
                   jax.ShapeDtypeStruct((B, T, nq), BF16)],
        compiler_params=pltpu.CompilerParams(dimension_semantics=("arbitrary", "arbitrary"),
                                             vmem_limit_bytes=VMEM_LIMIT),
        name="outproj",
    )(attn, lru, x, wout_bf, gpm, g1, gpf, sh2, sc2, wq_bf)


NEG_INF = float("-inf")
BIG = 1e9
SORT_GROUP = 4


def _slab_argmax(vs, cs):
    while len(vs) > 1:
        nv, nc = [], []
        for i in range(0, len(vs) - 1, 2):
            take = vs[i + 1] > vs[i]
            nv.append(jnp.where(take, vs[i + 1], vs[i]))
            nc.append(jnp.where(take, cs[i + 1], cs[i]))
        if len(vs) % 2:
            nv.append(vs[-1])
            nc.append(cs[-1])
        vs, cs = nv, nc
    return vs[0], cs[0]


def _extract_topk(vs, cs, k):
    n = vs[0].shape[1]
    slot = lax.broadcasted_iota(jnp.int32, (k, n), 0)
    top_v = jnp.zeros((k, n), F32)
    top_c = jnp.zeros((k, n), F32)
    groups = []
    for i in range(0, len(vs), SORT_GROUP):
        gv, gc = list(vs[i:i + SORT_GROUP]), list(cs[i:i + SORT_GROUP])
        for end in range(len(gv) - 1, 0, -1):
            for a in range(end):
                take = gv[a + 1] > gv[a]
                gv[a], gv[a + 1] = jnp.where(take, gv[a + 1], gv[a]), jnp.where(take, gv[a], gv[a + 1])
                gc[a], gc[a + 1] = jnp.where(take, gc[a + 1], gc[a]), jnp.where(take, gc[a], gc[a + 1])
        groups.append((gv, gc))
    for j in range(k):
        v, c = _slab_argmax([gv[0] for gv, _ in groups], [gc[0] for _, gc in groups])
        m = jnp.max(v, axis=0, keepdims=True)
        cm = jnp.min(jnp.where(v == m, c, BIG), axis=0, keepdims=True)
        top_v = jnp.where(slot == j, m, top_v)
        top_c = jnp.where(slot == j, cm, top_c)
        for gv, gc in groups:
            hit = gc[0] == cm
            for d in range(len(gv) - 1):
                gv[d] = jnp.where(hit, gv[d + 1], gv[d])
                gc[d] = jnp.where(hit, gc[d + 1], gc[d])
            gv[-1] = jnp.where(hit, NEG_INF, gv[-1])
    return top_v, top_c


def _lookup(table, pos):
    out = jnp.zeros(pos.shape, table.dtype)
    for i in range(PEER_TOPK):
        out = jnp.where(pos == i, table[i:i + 1, :], out)
    return out


def _topk_kernel(qp_ref, keys_ref, ia_ref, ib_ref, g_ref, *, tn):
    half = SUBLANES
    sub = lax.broadcasted_iota(jnp.int32, (half, tn), 0).astype(F32)
    key_codes = [sub + float(half * s) for s in range(PEER_NKEYS // half)]
    cand_codes = [sub, sub + float(half)] + [sub + 16.0 * i for i in range(1, half)] + [(sub + float(half)) * 16.0]
    for h in range(PEER_HEADS):
        tops = []
        for p in range(2):
            col = (2 * h + p) * PEER_NKEYS
            s = _dot_nt(keys_ref[2 * h + p], qp_ref[:, col:col + PEER_NKEYS])
            tops.append(_extract_topk([s[half * r:half * (r + 1), :] for r in range(PEER_NKEYS // half)],
                                      key_codes, PEER_TOPK))
        (v1, i1), (v2, i2) = tops
        groups = [v1[0:1, :] + v2[:half, :], v1[0:1, :] + v2[half:, :]]
        groups += [v1[i:i + 1, :] + v2[:half, :] for i in range(1, half)]
        groups += [v1[half:, :] + v2[0:1, :]]
        top_s, top_c = _extract_topk(groups, cand_codes, PEER_TOPK)
        e = jnp.exp(top_s - top_s[0:1, :])
        rows = slice(h * PEER_TOPK, (h + 1) * PEER_TOPK)
        g_ref[rows, :] = e / jnp.sum(e, axis=0, keepdims=True)
        pair = top_c.astype(jnp.int32)
        ia_ref[rows, :] = _lookup(i1, pair >> 4).astype(jnp.int32)
        ib_ref[rows, :] = _lookup(i2, pair & 15).astype(jnp.int32)


def _peer_topk(qp, keys_bf, tn):
    n, nq = qp.shape
    out = pl.BlockSpec((N_SEL, tn), lambda i: (0, i))
    return pl.pallas_call(
        functools.partial(_topk_kernel, tn=tn),
        grid=(n // tn,),
        in_specs=[pl.BlockSpec((tn, nq), lambda i: (i, 0)), _full(keys_bf.shape)],
        out_specs=[out, out, out],
        out_shape=[jax.ShapeDtypeStruct((N_SEL, n), jnp.int32), jax.ShapeDtypeStruct((N_SEL, n), jnp.int32),
                   jax.ShapeDtypeStruct((N_SEL, n), F32)],
        compiler_params=pltpu.CompilerParams(dimension_semantics=("arbitrary",), vmem_limit_bytes=VMEM_LIMIT),
        name="peer_topk",
    )(qp, keys_bf)


EXPERT_CHUNK = 2048
PEER_VMEM_LIMIT = 62 * 1024 * 1024
GROUPS_PER_CHUNK = EXPERT_CHUNK // PEER_NKEYS
TOKEN_UNROLL = 128
GROUP_PAD = 4


def _peer_kernel(hx_ref, ia_ref, ib_ref, g_ref, ut_ref, v_ref, xmid_ref, g2_ref, gpost_ref, o_ref,
                 s_ref, *, tb, n_chunks):
    j = pl.program_id(1)
    srow = tb + GROUP_PAD

    def group_rows(chunk, l):
        return pl.ds(pl.multiple_of(chunk * (GROUPS_PER_CHUNK * srow), SUBLANES) + l * srow, tb)

    @pl.when(j < n_chunks)
    def _scores():
        act = _gelu_tanh(_dot(hx_ref[...], ut_ref[0]))
        for l in range(GROUPS_PER_CHUNK):
            s_ref[group_rows(j, l), :] = act[:, l * PEER_NKEYS:(l + 1) * PEER_NKEYS]

    @pl.when(j == n_chunks)
    def _select():
        sub = lax.broadcasted_iota(jnp.int32, (PEER_NKEYS, N_SEL), 0)

        def tokens(tt, carry):
            for u in range(TOKEN_UNROLL):
                t = tt * TOKEN_UNROLL + u
                first = jnp.where(sub == ia_ref[t], g_ref[t], 0.0).astype(BF16)
                second = jnp.where(sub == ib_ref[t], 1.0, 0.0).astype(BF16)
                rows = pl.ds(t, PEER_NKEYS, stride=srow)
                s_ref[rows, :] = _dot_nt(first, second) * s_ref[rows, :]
            return carry

        lax.fori_loop(0, tb // TOKEN_UNROLL, tokens, 0)

    @pl.when(j >= n_chunks)
    def _combine():
        jj = j - n_chunks
        h = jnp.concatenate([s_ref[group_rows(jj, l), :] for l in range(GROUPS_PER_CHUNK)], axis=1)
        part = _dot(h.astype(BF16), v_ref[...])

        @pl.when(jj == 0)
        def _():
            o_ref[...] = part

        @pl.when(jj > 0)
        def _():
            o_ref[...] += part

    @pl.when(j == 2 * n_chunks - 1)
    def _finish():
        f = o_ref[...]
        o_ref[...] = xmid_ref[...] + g2_ref[0] * (f * _rms_scale(f) * gpost_ref[...])


def _peer_dense(hx, ia3, ib3, g3, ut_bf, v_bf, xmid, g2, gpost, tb, tokens_per_batch):
    n, d = hx.shape
    n_chunks = ut_bf.shape[0]
    bpb = tokens_per_batch // tb
    tok = pl.BlockSpec((tb, d), lambda i, j: (i, 0))
    tok_once = pl.BlockSpec((tb, d), lambda i, j: (i, 0), pipeline_mode=pl.Buffered(1))
    pick = pl.BlockSpec((tb, 1, N_SEL), lambda i, j: (i, 0, 0))
    return pl.pallas_call(
        functools.partial(_peer_kernel, tb=tb, n_chunks=n_chunks),
        grid=(n // tb, 2 * n_chunks),
        in_specs=[tok_once, pick, pick, pick,
                  pl.BlockSpec((1, d, EXPERT_CHUNK), lambda i, j: (jnp.minimum(j, n_chunks - 1), 0, 0)),
                  pl.BlockSpec((EXPERT_CHUNK, d), lambda i, j: (jnp.maximum(j - n_chunks, 0), 0)),
                  tok_once, pl.BlockSpec((1, 1, d), lambda i, j: (i // bpb, 0, 0)), _full((1, d))],
        out_specs=tok,
        out_shape=jax.ShapeDtypeStruct((n, d), F32),
        scratch_shapes=[pltpu.VMEM((PEER_NKEYS * (tb + GROUP_PAD), PEER_NKEYS), F32)],
        compiler_params=pltpu.CompilerParams(dimension_semantics=("arbitrary", "arbitrary"),
                                             vmem_limit_bytes=PEER_VMEM_LIMIT),
        name="peer_dense",
    )(hx, ia3, ib3, g3, ut_bf, v_bf, xmid, g2, gpost)


def _rope_tables(T):
    t = np.arange(T)
    inv = ROPE_THETA ** (-jnp.arange(ROPE_FREQS, dtype=F32) / ROPE_FREQS)
    ang_r = jnp.asarray(t // GRID_W, F32)[:, None] * inv
    ang_c = jnp.asarray(t % GRID_W, F32)[:, None] * inv
    cos = jnp.concatenate([jnp.cos(ang_r)] * 2 + [jnp.cos(ang_c)] * 2, axis=1)
    sin = jnp.concatenate([-jnp.sin(ang_r), jnp.sin(ang_r), -jnp.sin(ang_c), jnp.sin(ang_c)], axis=1)
    return jnp.concatenate([cos, cos], axis=1), jnp.concatenate([sin, sin], axis=1)


def _block_diag(w):
    n, m, _ = w.shape
    eye = jnp.eye(n, dtype=w.dtype)
    return (eye[:, None, :, None] * w[:, :, None, :]).reshape(n * m, n * m)


def kernel(x, c, ctx, c_ctx, w_mod, b_mod, g_pre_mix, g_post_mix, g_pre_ffn, g_post_ffn, w_in, q_norm_g, k_norm_g, conv_w, conv_b, rg_wa, rg_ba, rg_wi, rg_bi, rg_lambda, w_out, peer_wq, peer_subkeys, peer_u, peer_v):
    B, T, D = x.shape
    C = ctx.shape[1]
    depth = w_mod.shape[0]
    cos, sin = _rope_tables(T)
    head_ones = _block_diag(jnp.ones((N_Q_HEADS, HEAD_DIM, HEAD_DIM), BF16))
    mod_rows = -(-(B + 1) // SUBLANES) * SUBLANES

    for l in range(depth):
        assert l == depth - 1, "context-stream update between layers is not implemented"
        c_all = jnp.zeros((mod_rows, D), F32).at[:B].set(c).at[B].set(c_ctx)
        mod = _modulation(c_all, w_mod[l], b_mod[l])
        mx = mod[:B].reshape(B, N_MOD, 1, D)
        sh1, sc1, g1, sh2, sc2, g2 = (mx[:, i] for i in range(N_MOD))
        mc = mod[B].reshape(N_MOD, 1, D)

        w_bf = w_in[l].astype(BF16)
        qg = jnp.tile(q_norm_g[l], N_Q_HEADS).reshape(1, D_ATTN)
        kg = jnp.tile(k_norm_g[l], N_KV_HEADS).reshape(1, D_KV)
        gpre = g_pre_mix[l].reshape(1, D)
        q, kx, vx, lx, gz = _inproj_latent(x, sh1, sc1, gpre, w_bf, cos, sin, qg, kg, head_ones,
                                           tm=min(INPROJ_ROWS, T))
        kc, vc, lc = _inproj_context(ctx, mc[0], mc[1], gpre, w_bf[:, D_ATTN:D_ATTN + 2 * D_KV + D_LRU],
                                     kg, head_ones[:D_KV, :D_KV], tm=C)

        lru = _rglru(lc, lx, gz, conv_w[l], conv_b[l].reshape(1, D_LRU),
                     jnp.stack([_block_diag(rg_wa[l, d]) for d in range(2)]).astype(BF16),
                     jnp.stack([_block_diag(rg_wi[l, d]) for d in range(2)]).astype(BF16),
                     rg_ba[l].reshape(2, 1, D_LRU), rg_bi[l].reshape(2, 1, D_LRU), rg_lambda[l].reshape(2, 1, D_LRU))

        attn = _attention(q, jnp.concatenate([kc, kx], axis=2), jnp.concatenate([vc, vx], axis=2),
                          tq=min(ATTN_QUERIES, T))

        x_mid, hx, qp = _outproj(attn, lru, x, w_out[l].astype(BF16), g_post_mix[l].reshape(1, D), g1,
                                 g_pre_ffn[l].reshape(1, D), sh2, sc2, peer_wq[l].astype(BF16),
                                 tm=min(OUTPROJ_ROWS, T))

        n = B * T
        keys_bf = peer_subkeys[l].reshape(2 * PEER_HEADS, PEER_NKEYS, -1).astype(BF16)
        ia, ib, gate = _peer_topk(qp.reshape(n, -1), keys_bf, tn=TOPK_TOKENS)
        picks = lambda a: a.T.reshape(n, 1, N_SEL)
        ut = peer_u[l].astype(BF16).reshape(-1, EXPERT_CHUNK, D).transpose(0, 2, 1)
        x = _peer_dense(hx.reshape(n, D), picks(ia), picks(ib), picks(gate), ut,
                        peer_v[l].astype(BF16), x_mid.reshape(n, D), g2, g_post_ffn[l].reshape(1, D),
                        tb=min(PEER_TOKENS, T), tokens_per_batch=T).reshape(B, T, D)
    return x
```
